```python
import math
import jax, jax.numpy as jnp
from jax import lax
import numpy as np

D_MODEL = 4096
BATCH = 16
SEQ = 256
DEPTH = 2
DEC_BATCH = 2
DEC_SEQ = 2048
PAST_LEN = 256

GRID_W = 64
HEAD_DIM = 128
ATTN_W = D_MODEL // 2
N_HEADS = ATTN_W // HEAD_DIM
N_KV_HEADS = N_HEADS // 4
KV_W = N_KV_HEADS * HEAD_DIM
SSM_W = D_MODEL // 4
SSM_GROUP = 16
SSM_GROUPS = SSM_W // SSM_GROUP
SSM_STATE = 64
N_DIR = 2
CONV_W = D_MODEL // 4
CONV_K = 3
MIX_W = ATTN_W + SSM_W + CONV_W
IN_W = ATTN_W + 2 * KV_W + SSM_W + 3 * CONV_W
D_FF = 256 * ((8 * D_MODEL // 3 + 255) // 256)
N_EXPERTS = 8
TOP_K = 2
D_EXP = D_FF // 4
N_DENSE = (DEPTH + 1) // 2
N_MOE = DEPTH // 2
Q_BLOCK = 128
ROPE_THETA = 10000.0
EPS = 1e-6
N_MOD = 6

kernel_name = 'hybrid_prefix_diffusion_step'

F32 = jnp.float32


def rmsnorm(x, g):
    xf = x.astype(F32)
    xf = xf * lax.rsqrt(jnp.mean(xf * xf, axis=-1, keepdims=True) + EPS)
    return (xf * g.astype(F32)).astype(x.dtype)


def grid_rope(x):
    L = x.shape[1]
    rows = L // GRID_W
    row = jnp.repeat(jnp.arange(rows), GRID_W).astype(F32)
    col = jnp.tile(jnp.arange(GRID_W), rows).astype(F32)
    half = HEAD_DIM // 2
    inv = ROPE_THETA ** (-jnp.arange(0, half, 2, dtype=F32) / half)

    def rot(xa, pos):
        ang = pos[:, None] * inv[None, :]
        cos = jnp.cos(ang)[None, :, None, :]
        sin = jnp.sin(ang)[None, :, None, :]
        x1, x2 = jnp.split(xa.astype(F32), 2, axis=-1)
        return jnp.concatenate([x1 * cos - x2 * sin, x2 * cos + x1 * sin], axis=-1)

    out = jnp.concatenate([rot(x[..., :half], row), rot(x[..., half:], col)], axis=-1)
    return out.astype(x.dtype)


def block_attention(q, k, v):
    B, Lq = q.shape[0], q.shape[1]
    nblk = Lq // Q_BLOCK
    G = N_HEADS // N_KV_HEADS
    qb = q.reshape(B, nblk, Q_BLOCK, N_KV_HEADS, G, HEAD_DIM).transpose(1, 0, 2, 3, 4, 5)
    scale = HEAD_DIM ** -0.5

    def one_block(qblk):
        s = jnp.einsum('bqkgd,bskd->bkgqs', qblk, k, preferred_element_type=F32) * scale
        p = jax.nn.softmax(s, axis=-1).astype(v.dtype)
        return jnp.einsum('bkgqs,bskd->bqkgd', p, v)

    o = lax.map(one_block, qb)
    return o.transpose(1, 0, 2, 3, 4, 5).reshape(B, Lq, N_HEADS * HEAD_DIM)


def ssm_discretise(a_re, a_im, log_dt, b_re, b_im):
    a = lax.complex(a_re.astype(F32), a_im.astype(F32))
    dt = jnp.exp(log_dt.astype(F32))[:, None]
    a_bar = jnp.exp(dt * a)
    b = lax.complex(b_re.astype(F32), b_im.astype(F32))
    b_bar = ((a_bar - 1.0) / a)[..., None] * b
    return a_bar, b_bar


def _scan_op(e1, e2):
    a1, b1 = e1
    a2, b2 = e2
    return a1 * a2, a2 * b1 + b2


def ssm_direction(ug, a_bar, b_bar, c_mat, h0, reverse):
    bu = jnp.einsum('blgc,gpc->blgp', ug.astype(jnp.complex64), b_bar)
    if h0 is not None:
        edge = -1 if reverse else 0
        bu = bu.at[:, edge].add(a_bar[None] * h0)
    a = jnp.broadcast_to(a_bar, bu.shape)
    _, h = lax.associative_scan(_scan_op, (a, bu), reverse=reverse, axis=1)
    y = jnp.real(jnp.einsum('blgp,gcp->blgc', h, c_mat))
    return y, h


def ssm_mixer(u, P, l, h0):
    B, L, _ = u.shape
    uf = u.astype(F32)
    ug = uf.reshape(B, L, SSM_GROUPS, SSM_GROUP)
    y = P['ssm_d'][l].astype(F32) * uf
    finals = []
    for d in range(N_DIR):
        a_bar, b_bar = ssm_discretise(P['ssm_a_re'][l, d], P['ssm_a_im'][l, d], P['ssm_log_dt'][l, d],
                                      P['ssm_b_re'][l, d], P['ssm_b_im'][l, d])
        c_mat = lax.complex(P['ssm_c_re'][l, d].astype(F32), P['ssm_c_im'][l, d].astype(F32))
        reverse = d == 1
        yd, h = ssm_direction(ug, a_bar, b_bar, c_mat, None if h0 is None else h0[:, d], reverse)
        y = y + yd.reshape(B, L, SSM_W)
        if h0 is None:
            finals.append(h[:, 0] if reverse else h[:, -1])
    z = jax.nn.gelu(y)
    z = z * jax.nn.sigmoid(z @ P['w_glu'][l].astype(F32) + P['b_glu'][l].astype(F32))
    return z.astype(u.dtype), finals


def conv_mixer(xin, gb, gc, w):
    z = gc * xin
    zp = jnp.pad(z, ((0, 0), (1, 1), (0, 0)))
    y = zp[:, :-2] * w[0] + zp[:, 1:-1] * w[1] + zp[:, 2:] * w[2]
    return gb * y


def dense_swiglu(h, P, i):
    return (jax.nn.silu(h @ P['w_ffn_gate'][i]) * (h @ P['w_ffn_up'][i])) @ P['w_ffn_down'][i]


def moe_swiglu(h, P, i):
    logits = jnp.einsum('bld,de->ble', h, P['w_router'][i], preferred_element_type=F32) + P['b_router'][i].astype(F32)
    top_val, top_idx = lax.top_k(logits, TOP_K)
    top_w = jax.nn.softmax(top_val, axis=-1)
    gates = jnp.sum(jax.nn.one_hot(top_idx, N_EXPERTS, dtype=F32) * top_w[..., None], axis=-2)
    hg = jnp.einsum('bld,edf->blef', h, P['w_exp_gate'][i])
    hu = jnp.einsum('bld,edf->blef', h, P['w_exp_up'][i])
    act = jax.nn.silu(hg) * hu * gates[..., None].astype(h.dtype)
    return jnp.einsum('blef,efd->bld', act, P['w_exp_down'][i])


def trunk_layer(x, cond, P, l, ctx):
    B, L, _ = x.shape
    mod = jax.nn.silu(cond) @ P['w_mod'][l] + P['b_mod'][l]
    sh_m, sc_m, gt_m, sh_f, sc_f, gt_f = [m[:, None, :] for m in jnp.split(mod, N_MOD, axis=-1)]
    h = rmsnorm(x, P['g_mix_pre'][l]) * (1 + sc_m) + sh_m
    proj = h @ P['w_in'][l]
    cuts = [ATTN_W, ATTN_W + KV_W, ATTN_W + 2 * KV_W, ATTN_W + 2 * KV_W + SSM_W,
            ATTN_W + 2 * KV_W + SSM_W + CONV_W, ATTN_W + 2 * KV_W + SSM_W + 2 * CONV_W]
    q, k, v, u, cx, cb, cc = jnp.split(proj, cuts, axis=-1)
    q = rmsnorm(q.reshape(B, L, N_HEADS, HEAD_DIM), P['g_q'][l])
    k = rmsnorm(k.reshape(B, L, N_KV_HEADS, HEAD_DIM), P['g_k'][l])
    v = v.reshape(B, L, N_KV_HEADS, HEAD_DIM)
    if ctx is None:
        attn = block_attention(q, k, v)
        ssm_out, finals = ssm_mixer(u, P, l, None)
        new_ctx = (k, v, finals)
    else:
        ck, cv, h0 = ctx
        q = grid_rope(q)
        k = grid_rope(k)
        attn = block_attention(q, jnp.concatenate([ck, k], axis=1), jnp.concatenate([cv, v], axis=1))
        ssm_out, _ = ssm_mixer(u, P, l, h0)
        new_ctx = None
    conv_out = conv_mixer(cx, cb, cc, P['conv_w'][l])
    mix = jnp.concatenate([attn, ssm_out, conv_out], axis=-1) @ P['w_out'][l]
    x = x + gt_m * rmsnorm(mix, P['g_mix_post'][l])
    h = rmsnorm(x, P['g_ffn_pre'][l]) * (1 + sc_f) + sh_f
    f = dense_swiglu(h, P, l // 2) if l % 2 == 0 else moe_swiglu(h, P, l // 2)
    x = x + gt_f * rmsnorm(f, P['g_ffn_post'][l])
    return x, new_ctx


def setup_inputs(seed: int = 0) -> dict:
    key = jax.random.key(seed)
    ks = iter(jax.random.split(key, 48))
    nrm = lambda shape, s=1.0: jax.random.normal(next(ks), shape, F32) * s
    gain = lambda shape: 1.0 + 0.02 * jax.random.normal(next(ks), shape, F32)
    n_idx = jnp.arange(SSM_STATE, dtype=F32)
    a_im = jnp.broadcast_to(math.pi * n_idx, (DEPTH, N_DIR, SSM_GROUPS, SSM_STATE)) + nrm((DEPTH, N_DIR, SSM_GROUPS, SSM_STATE), 0.01)
    return {
        'x_prompt': nrm((BATCH, SEQ, D_MODEL)),
        'x_sample': nrm((DEC_BATCH, DEC_SEQ, D_MODEL)),
        'c': nrm((DEC_BATCH, D_MODEL)),
        'c_ctx': nrm((D_MODEL,)),
        'cache_k': nrm((DEC_BATCH, DEPTH, PAST_LEN, N_KV_HEADS, HEAD_DIM)),
        'cache_v': nrm((DEC_BATCH, DEPTH, PAST_LEN, N_KV_HEADS, HEAD_DIM)),
        'state_ssm_re': nrm((DEC_BATCH, DEPTH, N_DIR, SSM_GROUPS, SSM_STATE)),
        'state_ssm_im': nrm((DEC_BATCH, DEPTH, N_DIR, SSM_GROUPS, SSM_STATE)),
        'w_mod': nrm((DEPTH, D_MODEL, N_MOD * D_MODEL), D_MODEL ** -0.5),
        'b_mod': nrm((DEPTH, N_MOD * D_MODEL), 0.01),
        'g_mix_pre': gain((DEPTH, D_MODEL)),
        'g_mix_post': gain((DEPTH, D_MODEL)),
        'g_ffn_pre': gain((DEPTH, D_MODEL)),
        'g_ffn_post': gain((DEPTH, D_MODEL)),
        'w_in': nrm((DEPTH, D_MODEL, IN_W), D_MODEL ** -0.5),
        'g_q': gain((DEPTH, HEAD_DIM)),
        'g_k': gain((DEPTH, HEAD_DIM)),
        'ssm_a_re': -0.5 + nrm((DEPTH, N_DIR, SSM_GROUPS, SSM_STATE), 0.01),
        'ssm_a_im': a_im,
        'ssm_log_dt': jax.random.uniform(next(ks), (DEPTH, N_DIR, SSM_GROUPS), F32, math.log(0.001), math.log(0.1)),
        'ssm_b_re': nrm((DEPTH, N_DIR, SSM_GROUPS, SSM_STATE, SSM_GROUP), SSM_GROUP ** -0.5),
        'ssm_b_im': nrm((DEPTH, N_DIR, SSM_GROUPS, SSM_STATE, SSM_GROUP), SSM_GROUP ** -0.5),
        'ssm_c_re': nrm((DEPTH, N_DIR, SSM_GROUPS, SSM_GROUP, SSM_STATE), SSM_STATE ** -0.5),
        'ssm_c_im': nrm((DEPTH, N_DIR, SSM_GROUPS, SSM_GROUP, SSM_STATE), SSM_STATE ** -0.5),
        'ssm_d': nrm((DEPTH, SSM_W)),
        'w_glu': nrm((DEPTH, SSM_W, SSM_W), SSM_W ** -0.5),
        'b_glu': nrm((DEPTH, SSM_W), 0.01),
        'conv_w': nrm((DEPTH, CONV_K, CONV_W), CONV_K ** -0.5),
        'w_out': nrm((DEPTH, MIX_W, D_MODEL), MIX_W ** -0.5),
        'w_ffn_gate': nrm((N_DENSE, D_MODEL, D_FF), D_MODEL ** -0.5),
        'w_ffn_up': nrm((N_DENSE, D_MODEL, D_FF), D_MODEL ** -0.5),
        'w_ffn_down': nrm((N_DENSE, D_FF, D_MODEL), D_FF ** -0.5),
        'w_router': nrm((N_MOE, D_MODEL, N_EXPERTS), D_MODEL ** -0.5),
        'b_router': nrm((N_MOE, N_EXPERTS), 0.01),
        'w_exp_gate': nrm((N_MOE, N_EXPERTS, D_MODEL, D_EXP), D_MODEL ** -0.5),
        'w_exp_up': nrm((N_MOE, N_EXPERTS, D_MODEL, D_EXP), D_MODEL ** -0.5),
        'w_exp_down': nrm((N_MOE, N_EXPERTS, D_EXP, D_MODEL), D_EXP ** -0.5),
    }


def reference(x_prompt, x_sample, c, c_ctx, cache_k, cache_v, state_ssm_re, state_ssm_im,
              w_mod, b_mod, g_mix_pre, g_mix_post, g_ffn_pre, g_ffn_post, w_in, g_q, g_k,
              ssm_a_re, ssm_a_im, ssm_log_dt, ssm_b_re, ssm_b_im, ssm_c_re, ssm_c_im, ssm_d,
              w_glu, b_glu, conv_w, w_out, w_ffn_gate, w_ffn_up, w_ffn_down,
              w_router, b_router, w_exp_gate, w_exp_up, w_exp_down):
    P = dict(w_mod=w_mod, b_mod=b_mod, g_mix_pre=g_mix_pre, g_mix_post=g_mix_post,
             g_ffn_pre=g_ffn_pre, g_ffn_post=g_ffn_post, w_in=w_in, g_q=g_q, g_k=g_k,
             ssm_a_re=ssm_a_re, ssm_a_im=ssm_a_im, ssm_log_dt=ssm_log_dt, ssm_b_re=ssm_b_re,
             ssm_b_im=ssm_b_im, ssm_c_re=ssm_c_re, ssm_c_im=ssm_c_im, ssm_d=ssm_d,
             w_glu=w_glu, b_glu=b_glu, conv_w=conv_w, w_out=w_out, w_ffn_gate=w_ffn_gate,
             w_ffn_up=w_ffn_up, w_ffn_down=w_ffn_down, w_router=w_router, b_router=b_router,
             w_exp_gate=w_exp_gate, w_exp_up=w_exp_up, w_exp_down=w_exp_down)

    xp = x_prompt
    cond_ctx = c_ctx[None, :]
    ks, vs, s_re, s_im = [], [], [], []
    for l in range(DEPTH):
        xp, (k, v, finals) = trunk_layer(xp, cond_ctx, P, l, None)
        ks.append(k)
        vs.append(v)
        st = jnp.stack(finals, axis=1)
        s_re.append(jnp.real(st))
        s_im.append(jnp.imag(st))
    new_cache_k = jnp.stack(ks, axis=1)
    new_cache_v = jnp.stack(vs, axis=1)
    new_state_ssm_re = jnp.stack(s_re, axis=1)
    new_state_ssm_im = jnp.stack(s_im, axis=1)

    xs = x_sample
    for l in range(DEPTH):
        h0 = lax.complex(state_ssm_re[:, l].astype(F32), state_ssm_im[:, l].astype(F32))
        xs, _ = trunk_layer(xs, c, P, l, (cache_k[:, l], cache_v[:, l], h0))

    return (xp, xs, new_cache_k, new_cache_v, new_state_ssm_re, new_state_ssm_im)
```

```python
import functools
import math

import jax
import jax.numpy as jnp
from jax import lax
from jax.experimental import pallas as pl
from jax.experimental.pallas import tpu as pltpu

F32 = jnp.float32
BF16 = jnp.bfloat16

HEAD_DIM = 128
Q_PER_KV = 4
GRID_W = 64
SSM_GROUP = 16
N_MOD = 6
TOP_K = 2
ROPE_THETA = 10000.0
EPS = 1e-6

LANES = 128
MXU_DIM = 256
VMEM_LIMIT_BYTES = 56 * 1024 * 1024

SSM_CHUNK = 8
SSM_GROUPS_PER_BLOCK = LANES // SSM_GROUP


def _params(*semantics):
    return pltpu.CompilerParams(dimension_semantics=semantics, vmem_limit_bytes=VMEM_LIMIT_BYTES)


def _rms(x, g):
    return x * lax.rsqrt(jnp.mean(x * x, axis=-1, keepdims=True) + EPS) * g


def _silu(x):
    return x * jax.nn.sigmoid(x)


def _bdot(a, b):
    return jnp.dot(a, b, preferred_element_type=F32)


class _Tokens:
    def __init__(self, n_ctx_seq, ctx_len, n_lat_seq, lat_len):
        self.n_ctx_seq, self.ctx_len = n_ctx_seq, ctx_len
        self.n_lat_seq, self.lat_len = n_lat_seq, lat_len
        self.n_ctx = n_ctx_seq * ctx_len
        self.n_lat = n_lat_seq * lat_len
        self.total = self.n_ctx + self.n_lat

    def cond_row(self, i, tile):
        ctx_tiles = self.n_ctx // tile
        lat_tiles = self.lat_len // tile
        return jnp.where(i < ctx_tiles, 0, 1 + (i - ctx_tiles) // lat_tiles)


def _mod_kernel(c_ref, w_ref, b_ref, o_ref):
    s = _silu(c_ref[...]).astype(BF16)
    o_ref[...] = _bdot(s, w_ref[...].astype(BF16)) + b_ref[...]


def _modulation(cond, w_mod, b_mod):
    depth, d, n = w_mod.shape
    rows = cond.shape[0]
    tn = 512
    return pl.pallas_call(
        _mod_kernel,
        grid=(depth, n // tn),
        in_specs=[
            pl.BlockSpec((rows, d), lambda l, j: (0, 0)),
            pl.BlockSpec((None, d, tn), lambda l, j: (l, 0, j)),
            pl.BlockSpec((None, 1, tn), lambda l, j: (l, 0, j)),
        ],
        out_specs=pl.BlockSpec((None, rows, tn), lambda l, j: (l, 0, j)),
        out_shape=jax.ShapeDtypeStruct((depth, rows, n), F32),
        compiler_params=_params("parallel", "parallel"),
        name="modulation",
    )(cond, w_mod, b_mod.reshape(depth, 1, n))


NORM_TILE = 256


def _mod_spec(tok, layer, seg, d):
    return pl.BlockSpec((None, None, None, 1, d),
                        lambda i: (layer, tok.cond_row(i, NORM_TILE), seg, 0, 0))


def _prenorm_kernel(x_ref, g_ref, sc_ref, sh_ref, h_ref):
    h = _rms(x_ref[...], g_ref[...]) * (1.0 + sc_ref[...]) + sh_ref[...]
    h_ref[...] = h.astype(BF16)


def _prenorm(tok, x, g, mod5, layer, seg_shift, seg_scale):
    t, d = x.shape
    row = pl.BlockSpec((NORM_TILE, d), lambda i: (i, 0))
    return pl.pallas_call(
        _prenorm_kernel,
        grid=(t // NORM_TILE,),
        in_specs=[row, pl.BlockSpec((None, 1, d), lambda i: (layer, 0, 0)),
                  _mod_spec(tok, layer, seg_scale, d), _mod_spec(tok, layer, seg_shift, d)],
        out_specs=row,
        out_shape=jax.ShapeDtypeStruct((t, d), BF16),
        compiler_params=_params("parallel"),
        name="prenorm",
    )(x, g.reshape(g.shape[0], 1, d), mod5, mod5)


def _resid_norm_kernel(x_ref, y_ref, gpost_ref, gate_ref, *rest, with_next):
    x = x_ref[...] + gate_ref[...] * _rms(y_ref[...], gpost_ref[...])
    if with_next:
        gpre_ref, sc_ref, sh_ref, xo_ref, h_ref = rest
        h = _rms(x, gpre_ref[...]) * (1.0 + sc_ref[...]) + sh_ref[...]
        h_ref[...] = h.astype(BF16)
    else:
        (xo_ref,) = rest
    xo_ref[...] = x


def _resid_norm(tok, x, y, g_post, mod5, layer, seg_gate, nxt=None):
    t, d = x.shape
    row = pl.BlockSpec((NORM_TILE, d), lambda i: (i, 0))
    in_specs = [row, row, pl.BlockSpec((None, 1, d), lambda i: (layer, 0, 0)),
                _mod_spec(tok, layer, seg_gate, d)]
    args = [x, y, g_post.reshape(g_post.shape[0], 1, d), mod5]
    out_specs = [row]
    out_shape = [jax.ShapeDtypeStruct((t, d), F32)]
    if nxt is not None:
        g_pre, nl, seg_shift, seg_scale = nxt
        in_specs += [pl.BlockSpec((None, 1, d), lambda i: (nl, 0, 0)),
                     _mod_spec(tok, nl, seg_scale, d), _mod_spec(tok, nl, seg_shift, d)]
        args += [g_pre.reshape(g_pre.shape[0], 1, d), mod5, mod5]
        out_specs.append(row)
        out_shape.append(jax.ShapeDtypeStruct((t, d), BF16))
    return pl.pallas_call(
        functools.partial(_resid_norm_kernel, with_next=nxt is not None),
        grid=(t // NORM_TILE,),
        in_specs=in_specs, out_specs=out_specs, out_shape=out_shape,
        compiler_params=_params("parallel"),
        name="resid_norm",
    )(*args)


MM_TM = 1024
MM_TN = 512


def _mm_kernel(*refs, n_seg):
    o_ref = refs[-1]
    acc = None
    for s in range(n_seg):
        p = _bdot(refs[s][...], refs[n_seg + s][...].astype(BF16))
        acc = p if acc is None else acc + p
    o_ref[...] = acc.astype(o_ref.dtype)


def _matmul(segments, w, layer, out_dtype, tn=MM_TN):
    m = segments[0][0].shape[0]
    n = w.shape[-1]
    n_seg = len(segments)
    in_specs, args = [], []
    for x, _ in segments:
        in_specs.append(pl.BlockSpec((MM_TM, x.shape[1]), lambda i, j: (i, 0)))
        args.append(x)
    for x, rb in segments:
        in_specs.append(pl.BlockSpec((None, x.shape[1], tn), lambda i, j, rb=rb: (layer, rb, j)))
        args.append(w)
    return pl.pallas_call(
        functools.partial(_mm_kernel, n_seg=n_seg),
        grid=(m // MM_TM, n // tn),
        in_specs=in_specs,
        out_specs=pl.BlockSpec((MM_TM, tn), lambda i, j: (i, j)),
        out_shape=jax.ShapeDtypeStruct((m, n), out_dtype),
        compiler_params=_params("parallel", "parallel"),
        name="matmul",
    )(*args)


def _swiglu_up_kernel(x_ref, wg_ref, wu_ref, o_ref):
    x = x_ref[...]
    g = _bdot(x, wg_ref[...].astype(BF16))
    u = _bdot(x, wu_ref[...].astype(BF16))
    o_ref[...] = (_silu(g) * u).astype(BF16)


def _swiglu_up(h, w_gate, w_up, layer):
    m, d = h.shape
    n = w_gate.shape[-1]
    tn = 256
    wspec = pl.BlockSpec((None, d, tn), lambda i, j: (layer, 0, j))
    return pl.pallas_call(
        _swiglu_up_kernel,
        grid=(m // MM_TM, n // tn),
        in_specs=[pl.BlockSpec((MM_TM, d), lambda i, j: (i, 0)), wspec, wspec],
        out_specs=pl.BlockSpec((MM_TM, tn), lambda i, j: (i, j)),
        out_shape=jax.ShapeDtypeStruct((m, n), BF16),
        compiler_params=_params("parallel", "parallel"),
        name="swiglu_up",
    )(h, w_gate, w_up)


DOWN_TK = 256


def _down_kernel(x_ref, w_ref, o_ref, *, k_valid):
    k = pl.program_id(2)

    @pl.when(jnp.logical_and(pl.program_id(1) == 0, k == 0))
    def _():
        o_ref[...] = jnp.zeros(o_ref.shape, F32)

    w = w_ref[...]
    if k_valid % DOWN_TK:
        row = k * DOWN_TK + lax.broadcasted_iota(jnp.int32, (DOWN_TK, 1), 0)
        w = jnp.where(row < k_valid, w, 0.0)
    w = w.astype(BF16)
    x = x_ref[...]
    for c in range(0, o_ref.shape[1], MM_TN):
        o_ref[:, c:c + MM_TN] += _bdot(x, w[:, c:c + MM_TN])


def _down(act, w, layer):
    m = act.shape[0]
    n_grp, k_valid, n = w.shape[-3:]
    kt = pl.cdiv(k_valid, DOWN_TK)
    assert act.shape[1] == n_grp * kt * DOWN_TK
    return pl.pallas_call(
        functools.partial(_down_kernel, k_valid=k_valid),
        grid=(m // MM_TM, n_grp, kt),
        in_specs=[pl.BlockSpec((MM_TM, DOWN_TK), lambda i, g, k: (i, g * kt + k)),
                  pl.BlockSpec((None, None, DOWN_TK, n), lambda i, g, k: (layer, g, k, 0))],
        out_specs=pl.BlockSpec((MM_TM, n), lambda i, g, k: (i, 0)),
        out_shape=jax.ShapeDtypeStruct((m, n), F32),
        compiler_params=_params("parallel", "arbitrary", "arbitrary"),
        name="down",
    )(act, w)


ATTN_TILE = 256


def _rope_tables(tok):
    pos = jnp.arange(tok.lat_len)
    half = HEAD_DIM // 2
    inv = ROPE_THETA ** (-jnp.arange(0, half, 2, dtype=F32) / half)
    ang_r = (pos // GRID_W).astype(F32)[:, None] * inv[None, :]
    ang_c = (pos % GRID_W).astype(F32)[:, None] * inv[None, :]
    cos = jnp.concatenate([jnp.cos(ang_r)] * 2 + [jnp.cos(ang_c)] * 2, axis=-1)
    sin = jnp.concatenate([-jnp.sin(ang_r), jnp.sin(ang_r), -jnp.sin(ang_c), jnp.sin(ang_c)], axis=-1)
    ident = jnp.ones((ATTN_TILE, HEAD_DIM), F32)
    return (jnp.concatenate([ident, cos], axis=0), jnp.concatenate([0.0 * ident, sin], axis=0))


def _qkv_prep_kernel(q_ref, k_ref, v_ref, gq_ref, gk_ref, cos_ref, sin_ref,
                     qo_ref, kf_ref, kb_ref, vb_ref):
    cos, sin = cos_ref[...], sin_ref[...]
    lane = lax.broadcasted_iota(jnp.int32, cos.shape, 1)
    first = (lane % (HEAD_DIM // 2)) < (HEAD_DIM // 4)

    def norm_rope(x, g):
        xn = _rms(x, g)
        quarter = HEAD_DIM // 4
        partner = jnp.where(first, pltpu.roll(xn, HEAD_DIM - quarter, 1), pltpu.roll(xn, quarter, 1))
        return xn * cos + partner * sin

    gq, gk = gq_ref[...], gk_ref[...]
    for h in range(q_ref.shape[1] // HEAD_DIM):
        sl = slice(h * HEAD_DIM, (h + 1) * HEAD_DIM)
        qo_ref[:, sl] = norm_rope(q_ref[:, sl], gq).astype(BF16)
    for h in range(k_ref.shape[1] // HEAD_DIM):
        sl = slice(h * HEAD_DIM, (h + 1) * HEAD_DIM)
        kn = norm_rope(k_ref[:, sl], gk)
        kf_ref[:, sl] = kn
        kb_ref[:, sl] = kn.astype(BF16)
    vb_ref[...] = v_ref[...].astype(BF16)


def _qkv_prep(tok, proj, g_q, g_k, layer, attn_w, kv_w, cos, sin):
    t = proj.shape[0]
    tm = ATTN_TILE
    ctx_tiles = tok.n_ctx // tm
    lat_tiles = tok.lat_len // tm

    def table_row(i):
        return jnp.where(i < ctx_tiles, 0, 1 + (i - ctx_tiles) % lat_tiles)

    kv_blk = attn_w // kv_w
    gspec = pl.BlockSpec((None, 1, HEAD_DIM), lambda i: (layer, 0, 0))
    tspec = pl.BlockSpec((tm, HEAD_DIM), lambda i: (table_row(i), 0))
    kvspec = pl.BlockSpec((tm, kv_w), lambda i: (i, 0))
    return pl.pallas_call(
        _qkv_prep_kernel,
        grid=(t // tm,),
        in_specs=[pl.BlockSpec((tm, attn_w), lambda i: (i, 0)),
                  pl.BlockSpec((tm, kv_w), lambda i: (i, kv_blk)),
                  pl.BlockSpec((tm, kv_w), lambda i: (i, kv_blk + 1)),
                  gspec, gspec, tspec, tspec],
        out_specs=[pl.BlockSpec((tm, attn_w), lambda i: (i, 0)), kvspec, kvspec, kvspec],
        out_shape=[jax.ShapeDtypeStruct((t, attn_w), BF16), jax.ShapeDtypeStruct((t, kv_w), F32),
                   jax.ShapeDtypeStruct((t, kv_w), BF16), jax.ShapeDtypeStruct((t, kv_w), BF16)],
        compiler_params=_params("parallel"),
        name="qkv_prep",
    )(proj, proj, proj, g_q.reshape(-1, 1, HEAD_DIM), g_k.reshape(-1, 1, HEAD_DIM), cos, sin)


def _attn_kernel(q_ref, *refs, n_kv):
    o_ref = refs[-1]
    tq = q_ref.shape[0]
    q = jnp.concatenate([q_ref[:, g * HEAD_DIM:(g + 1) * HEAD_DIM] for g in range(Q_PER_KV)], axis=0)
    ks = [refs[2 * p][...].astype(BF16) for p in range(n_kv)]
    vs = [refs[2 * p + 1][...].astype(BF16) for p in range(n_kv)]
    ss = [lax.dot_general(q, k, (((1,), (1,)), ((), ())), preferred_element_type=F32) for k in ks]
    m = functools.reduce(jnp.maximum, [jnp.max(s, axis=-1, keepdims=True) for s in ss])
    scale = HEAD_DIM ** -0.5
    ps = [jnp.exp((s - m) * scale) for s in ss]
    denom = functools.reduce(jnp.add, [jnp.sum(p, axis=-1, keepdims=True) for p in ps])
    o = functools.reduce(jnp.add, [_bdot(p.astype(BF16), v) for p, v in zip(ps, vs)])
    o = o / denom
    for g in range(Q_PER_KV):
        o_ref[:, g * HEAD_DIM:(g + 1) * HEAD_DIM] = o[g * tq:(g + 1) * tq].astype(BF16)


def _attention_ctx(tok, qn, kb, vb):
    L = tok.ctx_len
    n_kv = kb.shape[1] // HEAD_DIM
    qw = Q_PER_KV * HEAD_DIM
    kvspec = pl.BlockSpec((L, HEAD_DIM), lambda s, h: (s, h))
    qspec = pl.BlockSpec((L, qw), lambda s, h: (s, h))
    return pl.pallas_call(
        functools.partial(_attn_kernel, n_kv=1),
        grid=(tok.n_ctx_seq, n_kv),
        in_specs=[qspec, kvspec, kvspec],
        out_specs=qspec,
        out_shape=jax.ShapeDtypeStruct((tok.n_ctx, qn.shape[1]), BF16),
        compiler_params=_params("parallel", "parallel"),
        name="attn_ctx",
    )(qn, kb, vb)


def _attention_lat(tok, qn, kb, vb, cache_k, cache_v, layer):
    L = tok.lat_len
    n_kv = kb.shape[1] // HEAD_DIM
    qw = Q_PER_KV * HEAD_DIM
    tq = ATTN_TILE
    q0 = tok.n_ctx // tq
    k0 = tok.n_ctx // L
    past = cache_k.shape[2]
    cspec = pl.BlockSpec((None, None, past, HEAD_DIM), lambda b, h, i: (b, layer, 0, h))
    kvspec = pl.BlockSpec((L, HEAD_DIM), lambda b, h, i: (k0 + b, h))
    return pl.pallas_call(
        functools.partial(_attn_kernel, n_kv=2),
        grid=(tok.n_lat_seq, n_kv, L // tq),
        in_specs=[pl.BlockSpec((tq, qw), lambda b, h, i: (q0 + b * (L // tq) + i, h)),
                  cspec, cspec, kvspec, kvspec],
        out_specs=pl.BlockSpec((tq, qw), lambda b, h, i: (b * (L // tq) + i, h)),
        out_shape=jax.ShapeDtypeStruct((tok.n_lat, qn.shape[1]), BF16),
        compiler_params=_params("parallel", "parallel", "parallel"),
        name="attn_lat",
    )(qn, cache_k, cache_v, kb, vb)


def _ssm_operators(a_re, a_im, log_dt, b_re, b_im, c_re, c_im):
    hi = lax.Precision.HIGHEST
    n_dir, G, P = a_re.shape
    Q, gb = SSM_CHUNK, SSM_GROUPS_PER_BLOCK
    nb = G // gb
    dt = jnp.exp(log_dt)[None, :, :, None]
    ks = jnp.arange(Q + 1, dtype=F32)[:, None, None, None]
    mag = jnp.exp(ks * dt * a_re[None])
    ang = ks * dt * a_im[None]
    pr, pi = mag * jnp.cos(ang), mag * jnp.sin(ang)
    nr, ni = pr[1] - 1.0, pi[1]
    den = a_re * a_re + a_im * a_im
    qr, qi = (nr * a_re + ni * a_im) / den, (ni * a_re - nr * a_im) / den
    bbr = qr[..., None] * b_re - qi[..., None] * b_im
    bbi = qr[..., None] * b_im + qi[..., None] * b_re
    pbr = pr[:Q, ..., None] * bbr[None] - pi[:Q, ..., None] * bbi[None]
    pbi = pr[:Q, ..., None] * bbi[None] + pi[:Q, ..., None] * bbr[None]
    kern = (jnp.einsum('dgcn,kdgni->kdgci', c_re, pbr, precision=hi)
            - jnp.einsum('dgcn,kdgni->kdgci', c_im, pbi, precision=hi))
    car = c_re[None] * pr[1:, :, :, None, :] - c_im[None] * pi[1:, :, :, None, :]
    cai = c_re[None] * pi[1:, :, :, None, :] + c_im[None] * pr[1:, :, :, None, :]

    eye = jnp.eye(gb, dtype=F32)
    s_idx = jnp.arange(Q)
    we, wy = [], []
    for d in range(n_dir):
        rev = d == 1
        lag_e = s_idx if rev else Q - 1 - s_idx
        lag_m = (s_idx[:, None] - s_idx[None, :]) if rev else (s_idx[None, :] - s_idx[:, None])
        lag_f = (Q - 1 - s_idx) if rev else s_idx
        er = pbr[lag_e, d].reshape(Q, nb, gb, P, SSM_GROUP)
        ei = pbi[lag_e, d].reshape(Q, nb, gb, P, SSM_GROUP)
        e2 = jnp.stack([er, ei], axis=0)
        we.append(jnp.einsum('rsjgni,gh->jsgirhn', e2, eye).reshape(nb, Q * LANES, 2 * gb * P))
        ksel = jnp.where((lag_m >= 0)[:, :, None, None, None],
                         kern[jnp.clip(lag_m, 0, Q - 1), d], 0.0)
        ksel = ksel.reshape(Q, Q, nb, gb, SSM_GROUP, SSM_GROUP)
        m_op = jnp.einsum('stjgoi,gh->jsgitho', ksel, eye).reshape(nb, Q * LANES, Q * LANES)
        f2 = jnp.stack([car[lag_f, d], -cai[lag_f, d]], axis=0)
        f2 = f2.reshape(2, Q, nb, gb, SSM_GROUP, P)
        f_op = jnp.einsum('rtjgon,gh->jrgntho', f2, eye).reshape(nb, 2 * gb * P, Q * LANES)
        wy.append(jnp.concatenate([m_op, f_op], axis=1))
    a_q = jnp.stack([pr[Q].reshape(n_dir, nb, 1, gb * P), pi[Q].reshape(n_dir, nb, 1, gb * P)], axis=3)
    a_q = a_q.reshape(n_dir, nb, 1, 2 * gb * P)
    return jnp.stack(we).astype(BF16), jnp.stack(wy).astype(BF16), a_q


def _ssm_kernel(u_ref, we_ref, wy_ref, aq_ref, h0_ref, y_ref, fin_ref, s_scr, hin_scr, *, tok):
    Q = SSM_CHUNK
    rows = u_ref.shape[0] // Q
    nlb = s_scr.shape[0]
    half = nlb // 2
    reverse = pl.program_id(0) == 1

    def lanes(b):
        return slice(b * LANES, (b + 1) * LANES)

    x = jnp.concatenate([u_ref[pl.ds(t, rows, stride=Q), :].astype(BF16) for t in range(Q)], axis=1)
    s = _bdot(x, we_ref[...])
    for b in range(nlb):
        s_scr[b] = s[:, lanes(b)]
    a = [aq_ref[:, lanes(b)] for b in range(nlb)]

    def scan(n_seq, n_chunk, row0, h_init):
        def step(i, h):
            c = jnp.where(reverse, n_chunk - 1 - i, i)
            sel = pl.ds(row0 + c, n_seq, stride=n_chunk)
            new = [None] * nlb
            for b in range(half):
                hr, hi = h[b], h[half + b]
                hin_scr[b, sel, :] = hr
                hin_scr[half + b, sel, :] = hi
                new[b] = a[b] * hr - a[half + b] * hi + s_scr[b, sel, :]
                new[half + b] = a[b] * hi + a[half + b] * hr + s_scr[half + b, sel, :]
            return tuple(new)
        return lax.fori_loop(0, n_chunk, step, h_init)

    n_ctx_rows = tok.n_ctx // Q
    zeros = tuple(jnp.zeros((tok.n_ctx_seq, LANES), F32) for _ in range(nlb))
    fin = scan(tok.n_ctx_seq, tok.ctx_len // Q, 0, zeros)
    for b in range(nlb):
        fin_ref[:, lanes(b)] = fin[b]
    scan(tok.n_lat_seq, tok.lat_len // Q, n_ctx_rows, tuple(h0_ref[:, lanes(b)] for b in range(nlb)))
    xin = jnp.concatenate([x] + [hin_scr[b].astype(BF16) for b in range(nlb)], axis=1)
    y = _bdot(xin, wy_ref[...])
    for t in range(Q):
        y_ref[pl.ds(t, rows, stride=Q), :] = y[:, lanes(t)]


def _ssm_scan(tok, proj, col_block0, we, wy, a_q, h0):
    n_dir, nb = we.shape[:2]
    t = proj.shape[0]
    rows = t // SSM_CHUNK
    sw = a_q.shape[-1]
    return pl.pallas_call(
        functools.partial(_ssm_kernel, tok=tok),
        grid=(n_dir, nb),
        in_specs=[pl.BlockSpec((t, LANES), lambda d, j: (0, col_block0 + j)),
                  pl.BlockSpec((None, None) + we.shape[2:], lambda d, j: (d, j, 0, 0)),
                  pl.BlockSpec((None, None) + wy.shape[2:], lambda d, j: (d, j, 0, 0)),
                  pl.BlockSpec((None, None, 1, sw), lambda d, j: (d, j, 0, 0)),
                  pl.BlockSpec((None, None, tok.n_lat_seq, sw), lambda d, j: (d, j, 0, 0))],
        out_specs=[pl.BlockSpec((None, t, LANES), lambda d, j: (d, 0, j)),
                   pl.BlockSpec((None, None, tok.n_ctx_seq, sw), lambda d, j: (d, j, 0, 0))],
        out_shape=[jax.ShapeDtypeStruct((n_dir, t, nb * LANES), F32),
                   jax.ShapeDtypeStruct((n_dir, nb, tok.n_ctx_seq, sw), F32)],
        scratch_shapes=[pltpu.VMEM((sw // LANES, rows, LANES), F32)] * 2,
        compiler_params=_params("arbitrary", "arbitrary"),
        name="ssm_scan",
    )(proj, we, wy, a_q, h0)


def _ssm_glu_kernel(u_ref, yf_ref, yb_ref, d_ref, w_ref, b_ref, o_ref):
    y = d_ref[...] * u_ref[...]
    y = y + yf_ref[...]
    y = y + yb_ref[...]
    z = jax.nn.gelu(y)
    gate = jax.nn.sigmoid(_bdot(z.astype(BF16), w_ref[...].astype(BF16)) + b_ref[...])
    o_ref[...] = (z * gate).astype(BF16)


def _ssm_glu(proj, col_block, ydir, ssm_d, w_glu, b_glu, layer):
    t = proj.shape[0]
    w = ydir.shape[-1]
    tm = 512
    row = pl.BlockSpec((None, 1, w), lambda i: (layer, 0, 0))
    return pl.pallas_call(
        _ssm_glu_kernel,
        grid=(t // tm,),
        in_specs=[pl.BlockSpec((tm, w), lambda i: (i, col_block)),
                  pl.BlockSpec((None, tm, w), lambda i: (0, i, 0)),
                  pl.BlockSpec((None, tm, w), lambda i: (1, i, 0)),
                  row, pl.BlockSpec((None, w, w), lambda i: (layer, 0, 0)), row],
        out_specs=pl.BlockSpec((tm, w), lambda i: (i, 0)),
        out_shape=jax.ShapeDtypeStruct((t, w), BF16),
        compiler_params=_params("parallel"),
        name="ssm_glu",
    )(proj, ydir, ydir, ssm_d.reshape(-1, 1, w), w_glu, b_glu.reshape(-1, 1, w))


CONV_TILE = 256
CONV_HALO = 8


def _conv_kernel(cx_ref, cb_ref, cc_ref, pcx_ref, pcc_ref, ncx_ref, ncc_ref, w_ref, o_ref, *, tok):
    i = pl.program_id(0)
    tm = CONV_TILE
    tiles_per_lat = tok.lat_len // tm
    ctx_tiles = tok.n_ctx // tm
    j = (i - ctx_tiles) % tiles_per_lat
    is_ctx = i < ctx_tiles
    has_prev = jnp.logical_and(jnp.logical_not(is_ctx), j > 0)
    has_next = jnp.logical_and(jnp.logical_not(is_ctx), j < tiles_per_lat - 1)
    z = cc_ref[...] * cx_ref[...]
    z_prev = jnp.where(has_prev, pcc_ref[CONV_HALO - 1:, :] * pcx_ref[CONV_HALO - 1:, :], 0.0)
    z_next = jnp.where(has_next, ncc_ref[:1, :] * ncx_ref[:1, :], 0.0)
    row = lax.broadcasted_iota(jnp.int32, z.shape, 0)
    up = jnp.where(row == 0, z_prev, pltpu.roll(z, 1, 0))
    dn = jnp.where(row == tm - 1, z_next, pltpu.roll(z, tm - 1, 0))
    y = up * w_ref[0:1, :] + z * w_ref[1:2, :] + dn * w_ref[2:3, :]
    o_ref[...] = (cb_ref[...] * y).astype(BF16)


def _conv(tok, proj, col_block0, conv_w, layer):
    t = proj.shape[0]
    w = conv_w.shape[-1]
    tm = CONV_TILE
    assert tok.ctx_len == tm and tok.lat_len % tm == 0
    hb = tm // CONV_HALO
    last = t // CONV_HALO - 1

    def main(c):
        return pl.BlockSpec((tm, w), lambda i: (i, col_block0 + c))

    def prev(c):
        return pl.BlockSpec((CONV_HALO, w), lambda i: (jnp.maximum(i * hb - 1, 0), col_block0 + c))

    def nxt(c):
        return pl.BlockSpec((CONV_HALO, w), lambda i: (jnp.minimum((i + 1) * hb, last), col_block0 + c))

    return pl.pallas_call(
        functools.partial(_conv_kernel, tok=tok),
        grid=(t // tm,),
        in_specs=[main(0), main(1), main(2), prev(0), prev(2), nxt(0), nxt(2),
                  pl.BlockSpec((None,) + conv_w.shape[1:], lambda i: (layer, 0, 0))],
        out_specs=pl.BlockSpec((tm, w), lambda i: (i, 0)),
        out_shape=jax.ShapeDtypeStruct((t, w), BF16),
        compiler_params=_params("parallel"),
        name="conv",
    )(proj, proj, proj, proj, proj, proj, proj, conv_w)


def _router_kernel(h_ref, w_ref, b_ref, g_ref):
    lg = _bdot(h_ref[...], w_ref[...].astype(BF16)) + b_ref[...]
    n_exp = float(lg.shape[-1])
    idx = lax.broadcasted_iota(jnp.int32, lg.shape, 1).astype(F32)
    m1 = jnp.max(lg, axis=-1, keepdims=True)
    i1 = jnp.min(jnp.where(lg == m1, idx, n_exp), axis=-1, keepdims=True)
    rest = jnp.where(idx == i1, -jnp.inf, lg)
    m2 = jnp.max(rest, axis=-1, keepdims=True)
    i2 = jnp.min(jnp.where(rest == m2, idx, n_exp), axis=-1, keepdims=True)
    e = jnp.exp(m2 - m1)
    w1 = 1.0 / (1.0 + e)
    g_ref[...] = jnp.where(idx == i1, w1, 0.0) + jnp.where(idx == i2, e * w1, 0.0)


def _router(h, w_router, b_router, layer):
    t, d = h.shape
    n_exp = w_router.shape[-1]
    tm = 512
    return pl.pallas_call(
        _router_kernel,
        grid=(t // tm,),
        in_specs=[pl.BlockSpec((tm, d), lambda i: (i, 0)),
                  pl.BlockSpec((None, d, n_exp), lambda i: (layer, 0, 0)),
                  pl.BlockSpec((None, 1, n_exp), lambda i: (layer, 0, 0))],
        out_specs=pl.BlockSpec((tm, n_exp), lambda i: (i, 0)),
        out_shape=jax.ShapeDtypeStruct((t, n_exp), F32),
        compiler_params=_params("parallel"),
        name="router",
    )(h, w_router, b_router.reshape(-1, 1, n_exp))


def _moe_up_kernel(x_ref, gates_ref, wg_ref, wu_ref, o_ref, *, d_exp):
    e, n = pl.program_id(0), pl.program_id(2)
    tn = o_ref.shape[-1]
    x = x_ref[...]
    g = _bdot(x, wg_ref[...].astype(BF16))
    u = _bdot(x, wu_ref[...].astype(BF16))
    gates = gates_ref[...]
    lane = lax.broadcasted_iota(jnp.int32, gates.shape, 1)
    gate = jnp.sum(jnp.where(lane == e, gates, 0.0), axis=-1, keepdims=True)
    col = n * tn + lax.broadcasted_iota(jnp.int32, (1, tn), 1)
    o_ref[...] = jnp.where(col < d_exp, _silu(g) * u * gate, 0.0).astype(BF16)


def _moe_up(h, gates, w_gate, w_up, layer):
    m, d = h.shape
    n_exp, _, d_exp = w_gate.shape[1:]
    tn = DOWN_TK
    nt = pl.cdiv(d_exp, tn)
    wspec = pl.BlockSpec((None, None, d, tn), lambda e, i, n: (layer, e, 0, n))
    return pl.pallas_call(
        functools.partial(_moe_up_kernel, d_exp=d_exp),
        grid=(n_exp, m // MM_TM, nt),
        in_specs=[pl.BlockSpec((MM_TM, d), lambda e, i, n: (i, 0)),
                  pl.BlockSpec((MM_TM, n_exp), lambda e, i, n: (i, 0)), wspec, wspec],
        out_specs=pl.BlockSpec((MM_TM, tn), lambda e, i, n: (i, e * nt + n)),
        out_shape=jax.ShapeDtypeStruct((m, n_exp * nt * tn), BF16),
        compiler_params=_params("parallel", "parallel", "parallel"),
        name="moe_up",
    )(h, gates, w_gate, w_up)


SEG_SHIFT_MIX, SEG_SCALE_MIX, SEG_GATE_MIX, SEG_SHIFT_FFN, SEG_SCALE_FFN, SEG_GATE_FFN = range(N_MOD)


def kernel(x_prompt, x_sample, c, c_ctx, cache_k, cache_v, state_ssm_re, state_ssm_im, w_mod, b_mod, g_mix_pre, g_mix_post, g_ffn_pre, g_ffn_post, w_in, g_q, g_k, ssm_a_re, ssm_a_im, ssm_log_dt, ssm_b_re, ssm_b_im, ssm_c_re, ssm_c_im, ssm_d, w_glu, b_glu, conv_w, w_out, w_ffn_gate, w_ffn_up, w_ffn_down, w_router, b_router, w_exp_gate, w_exp_up, w_exp_down):
    n_ctx_seq, ctx_len, d = x_prompt.shape
    n_lat_seq, lat_len, _ = x_sample.shape
    depth = w_in.shape[0]
    tok = _Tokens(n_ctx_seq, ctx_len, n_lat_seq, lat_len)
    attn_w = d // 2
    kv_w = attn_w // Q_PER_KV
    ssm_w = ssm_d.shape[-1]
    n_kv = kv_w // HEAD_DIM
    n_dir, n_grp, n_state = ssm_a_re.shape[1:]
    nb = n_grp // SSM_GROUPS_PER_BLOCK
    sw = 2 * SSM_GROUPS_PER_BLOCK * n_state

    x = jnp.concatenate([x_prompt.reshape(tok.n_ctx, d), x_sample.reshape(tok.n_lat, d)], axis=0)
    cond = jnp.concatenate([c_ctx[None, :], c, jnp.zeros((8 - 1 - n_lat_seq, d), F32)], axis=0)
    mod = _modulation(cond, w_mod, b_mod)
    mod5 = mod.reshape(depth, cond.shape[0], N_MOD, 1, d)
    cos, sin = _rope_tables(tok)
    ck = cache_k.reshape(cache_k.shape[:3] + (kv_w,))
    cv = cache_v.reshape(cache_v.shape[:3] + (kv_w,))

    h = _prenorm(tok, x, g_mix_pre, mod5, 0, SEG_SHIFT_MIX, SEG_SCALE_MIX)
    new_k, new_v, new_re, new_im = [], [], [], []
    for l in range(depth):
        proj = _matmul([(h, 0)], w_in, l, F32)
        qn, kf, kb, vb = _qkv_prep(tok, proj, g_q, g_k, l, attn_w, kv_w, cos, sin)
        attn = jnp.concatenate([_attention_ctx(tok, qn, kb, vb),
                                _attention_lat(tok, qn, kb, vb, ck, cv, l)], axis=0)
        new_k.append(kf[:tok.n_ctx].reshape(n_ctx_seq, ctx_len, n_kv, HEAD_DIM))
        new_v.append(proj[:tok.n_ctx, attn_w + kv_w:attn_w + 2 * kv_w].reshape(n_ctx_seq, ctx_len, n_kv, HEAD_DIM))

        u_col = attn_w + 2 * kv_w
        we, wy, a_q = _ssm_operators(ssm_a_re[l], ssm_a_im[l], ssm_log_dt[l], ssm_b_re[l], ssm_b_im[l],
                                     ssm_c_re[l], ssm_c_im[l])
        h0 = jnp.concatenate([state_ssm_re[:, l].reshape(n_lat_seq, n_dir, nb, sw // 2),
                              state_ssm_im[:, l].reshape(n_lat_seq, n_dir, nb, sw // 2)], axis=-1)
        ydir, fin = _ssm_scan(tok, proj, u_col // LANES, we, wy, a_q, h0.transpose(1, 2, 0, 3))
        fin = fin.transpose(2, 0, 1, 3).reshape(n_ctx_seq, n_dir, nb, 2, SSM_GROUPS_PER_BLOCK, n_state)
        new_re.append(fin[:, :, :, 0].reshape(n_ctx_seq, n_dir, n_grp, n_state))
        new_im.append(fin[:, :, :, 1].reshape(n_ctx_seq, n_dir, n_grp, n_state))
        ssm_out = _ssm_glu(proj, u_col // ssm_w, ydir, ssm_d, w_glu, b_glu, l)

        conv_out = _conv(tok, proj, (u_col + ssm_w) // ssm_w, conv_w, l)
        mix = _matmul([(attn, 0), (ssm_out, attn_w // ssm_w), (conv_out, attn_w // ssm_w + 1)], w_out, l, F32)
        x, h = _resid_norm(tok, x, mix, g_mix_post, mod5, l, SEG_GATE_MIX,
                           nxt=(g_ffn_pre, l, SEG_SHIFT_FFN, SEG_SCALE_FFN))
        if l % 2 == 0:
            act = _swiglu_up(h, w_ffn_gate, w_ffn_up, l // 2)
            f = _down(act, w_ffn_down.reshape((w_ffn_down.shape[0], 1) + w_ffn_down.shape[1:]), l // 2)
        else:
            gates = _router(h, w_router, b_router, l // 2)
            act = _moe_up(h, gates, w_exp_gate, w_exp_up, l // 2)
            f = _down(act, w_exp_down, l // 2)
        if l + 1 < depth:
            x, h = _resid_norm(tok, x, f, g_ffn_post, mod5, l, SEG_GATE_FFN,
                               nxt=(g_mix_pre, l + 1, SEG_SHIFT_MIX, SEG_SCALE_MIX))
        else:
            (x,) = _resid_norm(tok, x, f, g_ffn_post, mod5, l, SEG_GATE_FFN)

    return (x[:tok.n_ctx].reshape(x_prompt.shape), x[tok.n_ctx:].reshape(x_sample.shape),
            jnp.stack(new_k, axis=1), jnp.stack(new_v, axis=1),
            jnp.stack(new_re, axis=1), jnp.stack(new_im, axis=1))
```

```python
import functools
import math

import jax
import jax.numpy as jnp
from jax import lax
from jax.experimental import pallas as pl
from jax.experimental.pallas import tpu as pltpu

F32 = jnp.float32
BF16 = jnp.bfloat16

HEAD_DIM = 128
Q_PER_KV = 4
GRID_W = 64
SSM_GROUP = 16
N_MOD = 6
TOP_K = 2
ROPE_THETA = 10000.0
EPS = 1e-6

LANES = 128
MXU_DIM = 256
VMEM_LIMIT_BYTES = 56 * 1024 * 1024

SSM_CHUNK = 8
SSM_SCAN_UNROLL = 4
SSM_GROUPS_PER_BLOCK = LANES // SSM_GROUP


def _params(*semantics):
    return pltpu.CompilerParams(dimension_semantics=semantics, vmem_limit_bytes=VMEM_LIMIT_BYTES)


def _rms(x, g):
    return x * lax.rsqrt(jnp.mean(x * x, axis=-1, keepdims=True) + EPS) * g


def _silu(x):
    return x * jax.nn.sigmoid(x)


def _bdot(a, b):
    return jnp.dot(a, b, preferred_element_type=F32)


class _Tokens:
    def __init__(self, n_ctx_seq, ctx_len, n_lat_seq, lat_len):
        self.n_ctx_seq, self.ctx_len = n_ctx_seq, ctx_len
        self.n_lat_seq, self.lat_len = n_lat_seq, lat_len
        self.n_ctx = n_ctx_seq * ctx_len
        self.n_lat = n_lat_seq * lat_len
        self.total = self.n_ctx + self.n_lat

    def cond_row(self, i, tile):
        ctx_tiles = self.n_ctx // tile
        lat_tiles = self.lat_len // tile
        return jnp.where(i < ctx_tiles, 0, 1 + (i - ctx_tiles) // lat_tiles)


def _mod_kernel(c_ref, w_ref, b_ref, o_ref):
    s = _silu(c_ref[...]).astype(BF16)
    o_ref[...] = _bdot(s, w_ref[...].astype(BF16)) + b_ref[...]


def _modulation(cond, w_mod, b_mod):
    depth, d, n = w_mod.shape
    rows = cond.shape[0]
    tn = 512
    return pl.pallas_call(
        _mod_kernel,
        grid=(depth, n // tn),
        in_specs=[
            pl.BlockSpec((rows, d), lambda l, j: (0, 0)),
            pl.BlockSpec((None, d, tn), lambda l, j: (l, 0, j)),
            pl.BlockSpec((None, 1, tn), lambda l, j: (l, 0, j)),
        ],
        out_specs=pl.BlockSpec((None, rows, tn), lambda l, j: (l, 0, j)),
        out_shape=jax.ShapeDtypeStruct((depth, rows, n), F32),
        compiler_params=_params("parallel", "parallel"),
        name="modulation",
    )(cond, w_mod, b_mod.reshape(depth, 1, n))


NORM_TILE = 256


def _mod_spec(tok, layer, seg, d):
    return pl.BlockSpec((None, None, None, 1, d),
                        lambda i: (layer, tok.cond_row(i, NORM_TILE), seg, 0, 0))


def _split_row_specs(tok, d):
    ct = tok.n_ctx // NORM_TILE
    return [pl.BlockSpec((NORM_TILE, d), lambda i, *_: (jnp.minimum(i, ct - 1), 0)),
            pl.BlockSpec((NORM_TILE, d), lambda i, *_: (jnp.maximum(i - ct, 0), 0))]


def _split_shapes(tok, d):
    return [jax.ShapeDtypeStruct((tok.n_ctx, d), F32), jax.ShapeDtypeStruct((tok.n_lat, d), F32)]


def _load_rows(tok, refs):
    if len(refs) == 1:
        return refs[0][...]
    return jnp.where(pl.program_id(0) < tok.n_ctx // NORM_TILE, refs[0][...], refs[1][...])


def _store_rows(tok, refs, val):
    if len(refs) == 1:
        refs[0][...] = val
        return
    is_ctx = pl.program_id(0) < tok.n_ctx // NORM_TILE

    @pl.when(is_ctx)
    def _():
        refs[0][...] = val

    @pl.when(jnp.logical_not(is_ctx))
    def _():
        refs[1][...] = val


def _as_rows(x):
    return list(x) if isinstance(x, (tuple, list)) else [x]


def _prenorm_kernel(*refs, tok, n_x):
    g_ref, sc_ref, sh_ref, h_ref = refs[n_x:]
    h = _rms(_load_rows(tok, refs[:n_x]), g_ref[...]) * (1.0 + sc_ref[...]) + sh_ref[...]
    h_ref[...] = h.astype(BF16)


def _prenorm(tok, x, g, mod5, layer, seg_shift, seg_scale):
    xs = _as_rows(x)
    d = xs[0].shape[1]
    row = pl.BlockSpec((NORM_TILE, d), lambda i: (i, 0))
    return pl.pallas_call(
        functools.partial(_prenorm_kernel, tok=tok, n_x=len(xs)),
        grid=(tok.total // NORM_TILE,),
        in_specs=([row] if len(xs) == 1 else _split_row_specs(tok, d))
        + [pl.BlockSpec((None, 1, d), lambda i: (layer, 0, 0)),
           _mod_spec(tok, layer, seg_scale, d), _mod_spec(tok, layer, seg_shift, d)],
        out_specs=row,
        out_shape=jax.ShapeDtypeStruct((tok.total, d), BF16),
        compiler_params=_params("arbitrary"),
        name="prenorm",
    )(*xs, g.reshape(g.shape[0], 1, d), mod5, mod5)


def _resid_norm_kernel(*refs, tok, n_x, n_xo, with_next):
    y_ref, gpost_ref, gate_ref = refs[n_x:n_x + 3]
    rest = refs[n_x + 3:]
    x = _load_rows(tok, refs[:n_x]) + gate_ref[...] * _rms(y_ref[...], gpost_ref[...])
    if with_next:
        gpre_ref, sc_ref, sh_ref = rest[:3]
        h_ref = rest[-1]
        h = _rms(x, gpre_ref[...]) * (1.0 + sc_ref[...]) + sh_ref[...]
        h_ref[...] = h.astype(h_ref.dtype)
        rest = rest[3:-1]
    _store_rows(tok, rest[:n_xo], x)


def _resid_norm(tok, x, y, g_post, mod5, layer, seg_gate, nxt=None, h_dtype=BF16, split_out=False):
    xs = _as_rows(x)
    t, d = y.shape
    row = pl.BlockSpec((NORM_TILE, d), lambda i: (i, 0))
    in_specs = ([row] if len(xs) == 1 else _split_row_specs(tok, d)) + [
        row, pl.BlockSpec((None, 1, d), lambda i: (layer, 0, 0)), _mod_spec(tok, layer, seg_gate, d)]
    args = xs + [y, g_post.reshape(g_post.shape[0], 1, d), mod5]
    out_specs = _split_row_specs(tok, d) if split_out else [row]
    out_shape = _split_shapes(tok, d) if split_out else [jax.ShapeDtypeStruct((t, d), F32)]
    n_xo = len(out_specs)
    if nxt is not None:
        g_pre, nl, seg_shift, seg_scale = nxt
        in_specs += [pl.BlockSpec((None, 1, d), lambda i: (nl, 0, 0)),
                     _mod_spec(tok, nl, seg_scale, d), _mod_spec(tok, nl, seg_shift, d)]
        args += [g_pre.reshape(g_pre.shape[0], 1, d), mod5, mod5]
        out_specs.append(row)
        out_shape.append(jax.ShapeDtypeStruct((t, d), h_dtype))
    return pl.pallas_call(
        functools.partial(_resid_norm_kernel, tok=tok, n_x=len(xs), n_xo=n_xo, with_next=nxt is not None),
        grid=(t // NORM_TILE,),
        in_specs=in_specs, out_specs=out_specs, out_shape=out_shape,
        compiler_params=_params("arbitrary"),
        name="resid_norm",
    )(*args)


MM_TM = 1024
MM_TN = 512


def _mm_kernel(*refs, n_parts, ctx_tiles):
    o_ref = refs[-1]
    n_x = sum(n_parts)
    acc, at = None, 0
    for s, parts in enumerate(n_parts):
        x = refs[at][...]
        if parts == 2:
            x = jnp.where(pl.program_id(0) < ctx_tiles, x, refs[at + 1][...])
        at += parts
        p = _bdot(x, refs[n_x + s][...].astype(BF16))
        acc = p if acc is None else acc + p
    o_ref[...] = acc.astype(o_ref.dtype)


def _matmul(tok, segments, w, layer, out_dtype, tn=MM_TN):
    m = tok.total
    n = w.shape[-1]
    ct = tok.n_ctx // MM_TM
    in_specs, args, n_parts = [], [], []
    for x, _ in segments:
        xs = _as_rows(x)
        k = xs[0].shape[1]
        if len(xs) == 1:
            in_specs.append(pl.BlockSpec((MM_TM, k), lambda i, j: (i, 0)))
        else:
            in_specs += [pl.BlockSpec((MM_TM, k), lambda i, j: (jnp.minimum(i, ct - 1), 0)),
                         pl.BlockSpec((MM_TM, k), lambda i, j: (jnp.maximum(i - ct, 0), 0))]
        args += xs
        n_parts.append(len(xs))
    for x, rb in segments:
        k = _as_rows(x)[0].shape[1]
        in_specs.append(pl.BlockSpec((None, k, tn), lambda i, j, rb=rb: (layer, rb, j)))
        args.append(w)
    return pl.pallas_call(
        functools.partial(_mm_kernel, n_parts=tuple(n_parts), ctx_tiles=ct),
        grid=(m // MM_TM, n // tn),
        in_specs=in_specs,
        out_specs=pl.BlockSpec((MM_TM, tn), lambda i, j: (i, j)),
        out_shape=jax.ShapeDtypeStruct((m, n), out_dtype),
        compiler_params=_params("parallel", "parallel"),
        name="matmul",
    )(*args)


def _swiglu_up_kernel(x_ref, wg_ref, wu_ref, o_ref):
    x = x_ref[...]
    g = _bdot(x, wg_ref[...].astype(BF16))
    u = _bdot(x, wu_ref[...].astype(BF16))
    o_ref[...] = (_silu(g) * u).astype(BF16)


def _swiglu_up(h, w_gate, w_up, layer):
    m, d = h.shape
    n = w_gate.shape[-1]
    tn = 256
    wspec = pl.BlockSpec((None, d, tn), lambda i, j: (layer, 0, j))
    return pl.pallas_call(
        _swiglu_up_kernel,
        grid=(m // MM_TM, n // tn),
        in_specs=[pl.BlockSpec((MM_TM, d), lambda i, j: (i, 0)), wspec, wspec],
        out_specs=pl.BlockSpec((MM_TM, tn), lambda i, j: (i, j)),
        out_shape=jax.ShapeDtypeStruct((m, n), BF16),
        compiler_params=_params("parallel", "parallel"),
        name="swiglu_up",
    )(h, w_gate, w_up)


DOWN_TK = 256


def _down_kernel(x_ref, w_ref, o_ref, *, k_valid):
    k = pl.program_id(2)

    @pl.when(jnp.logical_and(pl.program_id(1) == 0, k == 0))
    def _():
        o_ref[...] = jnp.zeros(o_ref.shape, F32)

    w = w_ref[...]
    if k_valid % DOWN_TK:
        row = k * DOWN_TK + lax.broadcasted_iota(jnp.int32, (DOWN_TK, 1), 0)
        w = jnp.where(row < k_valid, w, 0.0)
    w = w.astype(BF16)
    x = x_ref[...]
    for c in range(0, o_ref.shape[1], MM_TN):
        o_ref[:, c:c + MM_TN] += _bdot(x, w[:, c:c + MM_TN])


def _down(act, w, layer):
    m = act.shape[0]
    n_grp, k_valid, n = w.shape[-3:]
    kt = pl.cdiv(k_valid, DOWN_TK)
    assert act.shape[1] == n_grp * kt * DOWN_TK
    return pl.pallas_call(
        functools.partial(_down_kernel, k_valid=k_valid),
        grid=(m // MM_TM, n_grp, kt),
        in_specs=[pl.BlockSpec((MM_TM, DOWN_TK), lambda i, g, k: (i, g * kt + k)),
                  pl.BlockSpec((None, None, DOWN_TK, n), lambda i, g, k: (layer, g, k, 0))],
        out_specs=pl.BlockSpec((MM_TM, n), lambda i, g, k: (i, 0)),
        out_shape=jax.ShapeDtypeStruct((m, n), F32),
        compiler_params=_params("parallel", "arbitrary", "arbitrary"),
        name="down",
    )(act, w)


ATTN_TILE = 256
CTX_HEADS_PER_CHAIN = 2
LAT_HEADS_PER_CHAIN = 1


def _rope_tables(tok):
    pos = jnp.arange(tok.lat_len)
    half = HEAD_DIM // 2
    inv = ROPE_THETA ** (-jnp.arange(0, half, 2, dtype=F32) / half)
    ang_r = (pos // GRID_W).astype(F32)[:, None] * inv[None, :]
    ang_c = (pos % GRID_W).astype(F32)[:, None] * inv[None, :]
    cos = jnp.concatenate([jnp.cos(ang_r)] * 2 + [jnp.cos(ang_c)] * 2, axis=-1)
    sin = jnp.concatenate([-jnp.sin(ang_r), jnp.sin(ang_r), -jnp.sin(ang_c), jnp.sin(ang_c)], axis=-1)
    ident = jnp.ones((ATTN_TILE, HEAD_DIM), F32)
    return (jnp.concatenate([ident, cos], axis=0), jnp.concatenate([0.0 * ident, sin], axis=0))


def _qkv_prep_kernel(q_ref, k_ref, v_ref, gq_ref, gk_ref, cos_ref, sin_ref,
                     qo_ref, kf_ref, kb_ref, vb_ref, *, ctx_tiles):
    def run(rope):
        if rope:
            cos, sin = cos_ref[...], sin_ref[...]
            lane = lax.broadcasted_iota(jnp.int32, cos.shape, 1)
            first = (lane % (HEAD_DIM // 2)) < (HEAD_DIM // 4)
            quarter = HEAD_DIM // 4

        def norm_rope(x, g):
            xn = _rms(x, g)
            if not rope:
                return xn
            partner = jnp.where(first, pltpu.roll(xn, HEAD_DIM - quarter, 1), pltpu.roll(xn, quarter, 1))
            return xn * cos + partner * sin

        gq, gk = gq_ref[...], gk_ref[...]
        for h in range(q_ref.shape[1] // HEAD_DIM):
            sl = slice(h * HEAD_DIM, (h + 1) * HEAD_DIM)
            qo_ref[:, sl] = norm_rope(q_ref[:, sl], gq).astype(BF16)
        for h in range(k_ref.shape[1] // HEAD_DIM):
            sl = slice(h * HEAD_DIM, (h + 1) * HEAD_DIM)
            kn = norm_rope(k_ref[:, sl], gk)
            kf_ref[:, sl] = kn
            kb_ref[:, sl] = kn.astype(BF16)

    is_ctx = pl.program_id(0) < ctx_tiles
    pl.when(is_ctx)(functools.partial(run, False))
    pl.when(jnp.logical_not(is_ctx))(functools.partial(run, True))
    vb_ref[...] = v_ref[...].astype(BF16)


def _qkv_prep(tok, proj, g_q, g_k, layer, attn_w, kv_w, cos, sin):
    t = proj.shape[0]
    tm = ATTN_TILE
    ctx_tiles = tok.n_ctx // tm
    lat_tiles = tok.lat_len // tm

    def table_row(i):
        return jnp.where(i < ctx_tiles, 0, 1 + (i - ctx_tiles) % lat_tiles)

    kv_blk = attn_w // kv_w
    gspec = pl.BlockSpec((None, 1, HEAD_DIM), lambda i: (layer, 0, 0))
    tspec = pl.BlockSpec((tm, HEAD_DIM), lambda i: (table_row(i), 0))
    kvspec = pl.BlockSpec((tm, kv_w), lambda i: (i, 0))
    return pl.pallas_call(
        functools.partial(_qkv_prep_kernel, ctx_tiles=ctx_tiles),
        grid=(t // tm,),
        in_specs=[pl.BlockSpec((tm, attn_w), lambda i: (i, 0)),
                  pl.BlockSpec((tm, kv_w), lambda i: (i, kv_blk)),
                  pl.BlockSpec((tm, kv_w), lambda i: (i, kv_blk + 1)),
                  gspec, gspec, tspec, tspec],
        out_specs=[pl.BlockSpec((tm, attn_w), lambda i: (i, 0)), kvspec, kvspec, kvspec],
        out_shape=[jax.ShapeDtypeStruct((t, attn_w), BF16), jax.ShapeDtypeStruct((t, kv_w), F32),
                   jax.ShapeDtypeStruct((t, kv_w), BF16), jax.ShapeDtypeStruct((t, kv_w), BF16)],
        compiler_params=_params("parallel"),
        name="qkv_prep",
    )(proj, proj, proj, g_q.reshape(-1, 1, HEAD_DIM), g_k.reshape(-1, 1, HEAD_DIM), cos, sin)


def _attn_kernel(q_ref, *refs, n_kv, heads_per_chain):
    o_ref = refs[-1]
    tq = q_ref.shape[0]
    ks = [refs[2 * p][...].astype(BF16) for p in range(n_kv)]
    vs = [refs[2 * p + 1][...].astype(BF16) for p in range(n_kv)]
    scale_log2e = HEAD_DIM ** -0.5 * math.log2(math.e)
    for g0 in range(0, Q_PER_KV, heads_per_chain):
        heads = range(g0, g0 + heads_per_chain)
        q = jnp.concatenate([q_ref[:, g * HEAD_DIM:(g + 1) * HEAD_DIM] for g in heads], axis=0)
        ss = [lax.dot_general(q, k, (((1,), (1,)), ((), ())), preferred_element_type=F32) for k in ks]
        m = functools.reduce(jnp.maximum, [jnp.max(s, axis=-1, keepdims=True) for s in ss])
        ps = [jnp.exp2((s - m) * scale_log2e) for s in ss]
        denom = functools.reduce(jnp.add, [jnp.sum(p, axis=-1, keepdims=True) for p in ps])
        o = functools.reduce(jnp.add, [_bdot(p.astype(BF16), v) for p, v in zip(ps, vs)])
        o = o / denom
        for n, g in enumerate(heads):
            o_ref[:, g * HEAD_DIM:(g + 1) * HEAD_DIM] = o[n * tq:(n + 1) * tq].astype(BF16)


def _attention_ctx(tok, qn, kb, vb):
    L = tok.ctx_len
    n_kv = kb.shape[1] // HEAD_DIM
    qw = Q_PER_KV * HEAD_DIM
    kvspec = pl.BlockSpec((L, HEAD_DIM), lambda s, h: (s, h))
    qspec = pl.BlockSpec((L, qw), lambda s, h: (s, h))
    return pl.pallas_call(
        functools.partial(_attn_kernel, n_kv=1, heads_per_chain=CTX_HEADS_PER_CHAIN),
        grid=(tok.n_ctx_seq, n_kv),
        in_specs=[qspec, kvspec, kvspec],
        out_specs=qspec,
        out_shape=jax.ShapeDtypeStruct((tok.n_ctx, qn.shape[1]), BF16),
        compiler_params=_params("parallel", "parallel"),
        name="attn_ctx",
    )(qn, kb, vb)


def _attention_lat(tok, qn, kb, vb, cache_k, cache_v, layer):
    L = tok.lat_len
    n_kv = kb.shape[1] // HEAD_DIM
    qw = Q_PER_KV * HEAD_DIM
    tq = ATTN_TILE
    q0 = tok.n_ctx // tq
    k0 = tok.n_ctx // L
    past = cache_k.shape[2]
    cspec = pl.BlockSpec((None, None, past, HEAD_DIM), lambda b, h, i: (b, layer, 0, h))
    kvspec = pl.BlockSpec((L, HEAD_DIM), lambda b, h, i: (k0 + b, h))
    return pl.pallas_call(
        functools.partial(_attn_kernel, n_kv=2, heads_per_chain=LAT_HEADS_PER_CHAIN),
        grid=(tok.n_lat_seq, n_kv, L // tq),
        in_specs=[pl.BlockSpec((tq, qw), lambda b, h, i: (q0 + b * (L // tq) + i, h)),
                  cspec, cspec, kvspec, kvspec],
        out_specs=pl.BlockSpec((tq, qw), lambda b, h, i: (b * (L // tq) + i, h)),
        out_shape=jax.ShapeDtypeStruct((tok.n_lat, qn.shape[1]), BF16),
        compiler_params=_params("parallel", "parallel", "parallel"),
        name="attn_lat",
    )(qn, cache_k, cache_v, kb, vb)


def _ssm_tables(a_re, a_im, log_dt, b_re, b_im, c_re, c_im):
    hi = lax.Precision.HIGHEST
    n_dir, G, P = a_re.shape
    Q, gb = SSM_CHUNK, SSM_GROUPS_PER_BLOCK
    nb = G // gb
    dt = jnp.exp(log_dt)[None, :, :, None]
    ks = jnp.arange(Q + 1, dtype=F32)[:, None, None, None]
    mag = jnp.exp(ks * dt * a_re[None])
    ang = ks * dt * a_im[None]
    pr, pi = mag * jnp.cos(ang), mag * jnp.sin(ang)
    nr, ni = pr[1] - 1.0, pi[1]
    den = a_re * a_re + a_im * a_im
    qr, qi = (nr * a_re + ni * a_im) / den, (ni * a_re - nr * a_im) / den
    bt_re, bt_im = jnp.swapaxes(b_re, -1, -2), jnp.swapaxes(b_im, -1, -2)
    bbr = qr[:, :, None, :] * bt_re - qi[:, :, None, :] * bt_im
    bbi = qr[:, :, None, :] * bt_im + qi[:, :, None, :] * bt_re
    pkr, pki = pr[:Q, :, :, None, :], pi[:Q, :, :, None, :]
    pbr = pkr * bbr[None] - pki * bbi[None]
    pbi = pkr * bbi[None] + pki * bbr[None]
    kern = (jnp.einsum('kdgin,dgon->kdgio', pbr, c_re, precision=hi)
            - jnp.einsum('kdgin,dgon->kdgio', pbi, c_im, precision=hi))
    pcr, pci = pr[1:, :, :, None, :], pi[1:, :, :, None, :]
    car = c_re[None] * pcr - c_im[None] * pci
    cai = c_re[None] * pci + c_im[None] * pcr

    def lead(t):
        return jnp.transpose(t, (1, 2, 0) + tuple(range(3, t.ndim)))

    eye = jnp.eye(gb, dtype=F32)
    kt = kern.reshape(Q, n_dir, nb, gb, SSM_GROUP, 1, SSM_GROUP) * eye[:, None, :, None]
    kt = lead(kt.reshape(Q, n_dir, nb, LANES, LANES))
    e2 = jnp.concatenate([pbr, pbr, pbi, pbi], axis=-1)
    e2 = lead(e2.reshape(Q, n_dir, nb, LANES, 4 * P))

    def state_major(t):
        t = t.reshape(Q, n_dir, nb, gb, SSM_GROUP, P)
        return jnp.transpose(t, (0, 1, 2, 5, 3, 4)).reshape(Q, n_dir, nb, P, LANES)

    fc = lead(jnp.concatenate([state_major(car), -state_major(cai)], axis=3))
    a_q = jnp.concatenate([pr[Q].reshape(n_dir, nb, 1, gb * P), pi[Q].reshape(n_dir, nb, 1, gb * P)], axis=-1)
    return kt, e2, fc, a_q


def _ssm_kernel(u_ref, kt_ref, e2_ref, fc_ref, aq_ref, h0_ref, y_ref, fin_ref,
                we_scr, wy_scr, s_scr, hin_scr, *, tok):
    Q = SSM_CHUNK
    rows = u_ref.shape[0] // Q
    nlb = s_scr.shape[0]
    half = nlb // 2
    sw = nlb * LANES
    n_state = fc_ref.shape[1] // 2
    reverse = pl.program_id(0) == 1

    def lanes(b):
        return slice(b * LANES, (b + 1) * LANES)

    row_g = lax.broadcasted_iota(jnp.int32, (LANES, sw), 0) // SSM_GROUP
    col_h = (lax.broadcasted_iota(jnp.int32, (LANES, sw), 1) % (sw // 2)) // n_state
    reps = sw // 2 // LANES
    for s in range(Q):
        e2 = e2_ref[jnp.where(reverse, s, Q - 1 - s)]
        tiled = jnp.concatenate([e2[:, :LANES]] * reps + [e2[:, LANES:]] * reps, axis=1)
        we_scr[lanes(s), :] = jnp.where(row_g == col_h, tiled, 0.0).astype(BF16)
    zero_blk = jnp.zeros((LANES, LANES), F32)
    for s in range(Q):
        for t in range(Q):
            if s == t:
                blk = kt_ref[0]
            elif t > s:
                blk = jnp.where(reverse, zero_blk, kt_ref[t - s])
            else:
                blk = jnp.where(reverse, kt_ref[s - t], zero_blk)
            wy_scr[lanes(s), lanes(t)] = blk.astype(BF16)
    row_h = (lax.broadcasted_iota(jnp.int32, (sw, LANES), 0) % (sw // 2)) // n_state
    col_g = lax.broadcasted_iota(jnp.int32, (sw, LANES), 1) // SSM_GROUP
    n_rep = sw // 2 // n_state
    for t in range(Q):
        fc = fc_ref[jnp.where(reverse, Q - 1 - t, t)]
        tiled = jnp.concatenate([fc[:n_state]] * n_rep + [fc[n_state:]] * n_rep, axis=0)
        wy_scr[Q * LANES:, lanes(t)] = jnp.where(row_h == col_g, tiled, 0.0).astype(BF16)

    x = jnp.concatenate([u_ref[pl.ds(t, rows, stride=Q), :].astype(BF16) for t in range(Q)], axis=1)
    s = _bdot(x, we_scr[...])
    for b in range(nlb):
        s_scr[b] = s[:, lanes(b)]
    a = [aq_ref[:, lanes(b)] for b in range(nlb)]

    def scan(n_seq, n_chunk, row0, h_init):
        def step(i, h):
            c = jnp.where(reverse, n_chunk - 1 - i, i)
            sel = pl.ds(row0 + c, n_seq, stride=n_chunk)
            new = [None] * nlb
            for b in range(half):
                hr, hi = h[b], h[half + b]
                hin_scr[b, sel, :] = hr
                hin_scr[half + b, sel, :] = hi
                new[b] = a[b] * hr - a[half + b] * hi + s_scr[b, sel, :]
                new[half + b] = a[b] * hi + a[half + b] * hr + s_scr[half + b, sel, :]
            return tuple(new)
        return lax.fori_loop(0, n_chunk, step, h_init, unroll=SSM_SCAN_UNROLL)

    n_ctx_rows = tok.n_ctx // Q
    zeros = tuple(jnp.zeros((tok.n_ctx_seq, LANES), F32) for _ in range(nlb))
    fin = scan(tok.n_ctx_seq, tok.ctx_len // Q, 0, zeros)
    for b in range(nlb):
        fin_ref[:, lanes(b)] = fin[b]
    scan(tok.n_lat_seq, tok.lat_len // Q, n_ctx_rows, tuple(h0_ref[:, lanes(b)] for b in range(nlb)))
    xin = jnp.concatenate([x] + [hin_scr[b].astype(BF16) for b in range(nlb)], axis=1)
    y = _bdot(xin, wy_scr[...])
    for t in range(Q):
        y_ref[pl.ds(t, rows, stride=Q), :] = y[:, lanes(t)]


def _ssm_scan(tok, proj, col_block0, kt, e2, fc, a_q, h0):
    n_dir, nb = kt.shape[:2]
    t = proj.shape[0]
    Q = SSM_CHUNK
    rows = t // Q
    sw = a_q.shape[-1]

    def table(x):
        return pl.BlockSpec((None, None) + x.shape[2:], lambda d, j: (d, j) + (0,) * (x.ndim - 2))

    return pl.pallas_call(
        functools.partial(_ssm_kernel, tok=tok),
        grid=(n_dir, nb),
        in_specs=[pl.BlockSpec((t, LANES), lambda d, j: (0, col_block0 + j)),
                  table(kt), table(e2), table(fc), table(a_q), table(h0)],
        out_specs=[pl.BlockSpec((None, t, LANES), lambda d, j: (d, 0, j)),
                   pl.BlockSpec((None, None, tok.n_ctx_seq, sw), lambda d, j: (d, j, 0, 0))],
        out_shape=[jax.ShapeDtypeStruct((n_dir, t, nb * LANES), F32),
                   jax.ShapeDtypeStruct((n_dir, nb, tok.n_ctx_seq, sw), F32)],
        scratch_shapes=[pltpu.VMEM((Q * LANES, sw), BF16), pltpu.VMEM((Q * LANES + sw, Q * LANES), BF16),
                        pltpu.VMEM((sw // LANES, rows, LANES), F32), pltpu.VMEM((sw // LANES, rows, LANES), F32)],
        compiler_params=_params("arbitrary", "arbitrary"),
        name="ssm_scan",
    )(proj, kt, e2, fc, a_q, h0)


def _ssm_glu_kernel(u_ref, yf_ref, yb_ref, d_ref, w_ref, b_ref, o_ref):
    y = d_ref[...] * u_ref[...]
    y = y + yf_ref[...]
    y = y + yb_ref[...]
    z = jax.nn.gelu(y)
    gate = jax.nn.sigmoid(_bdot(z.astype(BF16), w_ref[...].astype(BF16)) + b_ref[...])
    o_ref[...] = (z * gate).astype(BF16)


def _ssm_glu(proj, col_block, ydir, ssm_d, w_glu, b_glu, layer):
    t = proj.shape[0]
    w = ydir.shape[-1]
    tm = 512
    row = pl.BlockSpec((None, 1, w), lambda i: (layer, 0, 0))
    return pl.pallas_call(
        _ssm_glu_kernel,
        grid=(t // tm,),
        in_specs=[pl.BlockSpec((tm, w), lambda i: (i, col_block)),
                  pl.BlockSpec((None, tm, w), lambda i: (0, i, 0)),
                  pl.BlockSpec((None, tm, w), lambda i: (1, i, 0)),
                  row, pl.BlockSpec((None, w, w), lambda i: (layer, 0, 0)), row],
        out_specs=pl.BlockSpec((tm, w), lambda i: (i, 0)),
        out_shape=jax.ShapeDtypeStruct((t, w), BF16),
        compiler_params=_params("parallel"),
        name="ssm_glu",
    )(proj, ydir, ydir, ssm_d.reshape(-1, 1, w), w_glu, b_glu.reshape(-1, 1, w))


CONV_TILE = 256
CONV_HALO = 8


def _conv_kernel(cx_ref, cb_ref, cc_ref, pcx_ref, pcc_ref, ncx_ref, ncc_ref, w_ref, o_ref, *, tok):
    i = pl.program_id(0)
    tm = CONV_TILE
    tiles_per_lat = tok.lat_len // tm
    ctx_tiles = tok.n_ctx // tm
    j = (i - ctx_tiles) % tiles_per_lat
    is_ctx = i < ctx_tiles
    has_prev = jnp.logical_and(jnp.logical_not(is_ctx), j > 0)
    has_next = jnp.logical_and(jnp.logical_not(is_ctx), j < tiles_per_lat - 1)
    z = cc_ref[...] * cx_ref[...]
    z_prev = jnp.where(has_prev, pcc_ref[CONV_HALO - 1:, :] * pcx_ref[CONV_HALO - 1:, :], 0.0)
    z_next = jnp.where(has_next, ncc_ref[:1, :] * ncx_ref[:1, :], 0.0)
    row = lax.broadcasted_iota(jnp.int32, z.shape, 0)
    up = jnp.where(row == 0, z_prev, pltpu.roll(z, 1, 0))
    dn = jnp.where(row == tm - 1, z_next, pltpu.roll(z, tm - 1, 0))
    y = up * w_ref[0:1, :] + z * w_ref[1:2, :] + dn * w_ref[2:3, :]
    o_ref[...] = (cb_ref[...] * y).astype(BF16)


def _conv(tok, proj, col_block0, conv_w, layer):
    t = proj.shape[0]
    w = conv_w.shape[-1]
    tm = CONV_TILE
    assert tok.ctx_len == tm and tok.lat_len % tm == 0
    hb = tm // CONV_HALO
    last = t // CONV_HALO - 1

    def main(c):
        return pl.BlockSpec((tm, w), lambda i: (i, col_block0 + c))

    def prev(c):
        return pl.BlockSpec((CONV_HALO, w), lambda i: (jnp.maximum(i * hb - 1, 0), col_block0 + c))

    def nxt(c):
        return pl.BlockSpec((CONV_HALO, w), lambda i: (jnp.minimum((i + 1) * hb, last), col_block0 + c))

    return pl.pallas_call(
        functools.partial(_conv_kernel, tok=tok),
        grid=(t // tm,),
        in_specs=[main(0), main(1), main(2), prev(0), prev(2), nxt(0), nxt(2),
                  pl.BlockSpec((None,) + conv_w.shape[1:], lambda i: (layer, 0, 0))],
        out_specs=pl.BlockSpec((tm, w), lambda i: (i, 0)),
        out_shape=jax.ShapeDtypeStruct((t, w), BF16),
        compiler_params=_params("parallel"),
        name="conv",
    )(proj, proj, proj, proj, proj, proj, proj, conv_w)


MOE_TILE = 512
GATHER_ROWS = 256
N_DMA_QUEUES = 2
NT_DIMS = (((1,), (1,)), ((), ()))


def _router_kernel(h_ref, w_ref, b_ref, idx_ref, wt_ref):
    h, w = h_ref[...], w_ref[...]
    hh, wh = h.astype(BF16), w.astype(BF16)
    hl, wl = (h - hh.astype(F32)).astype(BF16), (w - wh.astype(F32)).astype(BF16)

    def nt(a, b):
        return lax.dot_general(a, b, NT_DIMS, preferred_element_type=F32)

    lg = nt(hh, wh) + nt(hl, wh) + nt(hh, wl) + b_ref[...]
    n_exp = float(lg.shape[-1])
    idx = lax.broadcasted_iota(jnp.int32, lg.shape, 1).astype(F32)
    m1 = jnp.max(lg, axis=-1, keepdims=True)
    i1 = jnp.min(jnp.where(lg == m1, idx, n_exp), axis=-1, keepdims=True)
    rest = jnp.where(idx == i1, -jnp.inf, lg)
    m2 = jnp.max(rest, axis=-1, keepdims=True)
    i2 = jnp.min(jnp.where(rest == m2, idx, n_exp), axis=-1, keepdims=True)
    e = jnp.exp(m2 - m1)
    w1 = 1.0 / (1.0 + e)
    slot = lax.broadcasted_iota(jnp.int32, idx_ref.shape, 1)
    idx_ref[...] = jnp.where(slot == 0, i1, i2).astype(jnp.int32)
    wt_ref[...] = jnp.where(slot == 0, w1, e * w1)


def _router(h, w_router_t, b_router, layer):
    t, d = h.shape
    n_exp = w_router_t.shape[1]
    tm = 512
    out = pl.BlockSpec((tm, TOP_K), lambda i: (i, 0))
    return pl.pallas_call(
        _router_kernel,
        grid=(t // tm,),
        in_specs=[pl.BlockSpec((tm, d), lambda i: (i, 0)),
                  pl.BlockSpec((None, n_exp, d), lambda i: (layer, 0, 0)),
                  pl.BlockSpec((None, 1, n_exp), lambda i: (layer, 0, 0))],
        out_specs=[out, out],
        out_shape=[jax.ShapeDtypeStruct((t, TOP_K), jnp.int32), jax.ShapeDtypeStruct((t, TOP_K), F32)],
        compiler_params=_params("parallel"),
        name="router",
    )(h, w_router_t, b_router.reshape(-1, 1, n_exp))


def _moe_plan(idx, n_exp):
    t, k = idx.shape
    e = idx.reshape(-1)
    onehot = (e[:, None] == jnp.arange(n_exp, dtype=jnp.int32)[None, :]).astype(jnp.int32)
    csum = jnp.cumsum(onehot, axis=0)
    counts = csum[-1]
    rank = jnp.take_along_axis(csum, e[:, None], axis=1)[:, 0] - 1
    tiles_e = (counts + MOE_TILE - 1) // MOE_TILE
    tile_end = jnp.cumsum(tiles_e)
    tile_start = tile_end - tiles_e
    n_used = tile_end[-1]
    dest = tile_start[e] * MOE_TILE + rank
    n_tiles = (t * k) // MOE_TILE + n_exp
    ti = jnp.arange(n_tiles, dtype=jnp.int32)
    tile_block = jnp.minimum(ti, n_used - 1)
    tile_expert = jnp.sum((tile_end[None, :] <= tile_block[:, None]).astype(jnp.int32), axis=1)
    tile_expert = jnp.minimum(tile_expert, n_exp - 1)
    order = jnp.argsort(e, stable=True).astype(jnp.int32)
    first = jnp.cumsum(counts) - counts
    r = jnp.arange(n_tiles * MOE_TILE, dtype=jnp.int32)
    er = tile_expert[r // MOE_TILE]
    off = r - tile_start[er] * MOE_TILE
    valid = jnp.logical_and(off < counts[er], r // MOE_TILE < n_used)
    src = jnp.where(valid, order[jnp.clip(first[er] + off, 0, t * k - 1)] // k, 0)
    return dest.astype(jnp.int32), src.astype(jnp.int32), tile_expert, tile_block.astype(jnp.int32), \
        n_used.reshape(1).astype(jnp.int32)


def _start_row_copies(idx_ref, base, n_rows, src_hbm, buf, slot, sem):
    def body(pair, carry):
        for queue in range(N_DMA_QUEUES):
            r = N_DMA_QUEUES * pair + queue
            row = idx_ref[base + r]
            pltpu.make_async_copy(src_hbm.at[pl.ds(row, 1)], buf.at[slot, pl.ds(r, 1)],
                                  sem.at[slot]).start(priority=queue)
        return carry
    lax.fori_loop(0, n_rows // N_DMA_QUEUES, body, 0, unroll=4)


def _wait_row_copies(src_hbm, buf, slot, sem):
    pltpu.make_async_copy(src_hbm.at[pl.ds(0, buf.shape[1])], buf.at[slot], sem.at[slot]).wait()


def _row_gather_pipeline(idx_ref, n_rows, src_hbm, buf, sem, n_steps):
    s = pl.program_id(0)
    slot = s % 2

    @pl.when(s == 0)
    def _():
        _start_row_copies(idx_ref, 0, n_rows, src_hbm, buf, 0, sem)

    @pl.when(s + 1 < n_steps)
    def _():
        _start_row_copies(idx_ref, (s + 1) * n_rows, n_rows, src_hbm, buf, 1 - slot, sem)

    @pl.when(s < n_steps)
    def _():
        _wait_row_copies(src_hbm, buf, slot, sem)

    return slot


def _gather_kernel(idx_ref, n_steps_ref, src_hbm, o_ref, buf, sem):
    n_steps = n_steps_ref[0]
    slot = _row_gather_pipeline(idx_ref, o_ref.shape[0], src_hbm, buf, sem, n_steps)
    used = pl.program_id(0) < n_steps

    @pl.when(used)
    def _():
        o_ref[...] = buf[slot].astype(o_ref.dtype)

    @pl.when(jnp.logical_not(used))
    def _():
        o_ref[...] = jnp.zeros(o_ref.shape, o_ref.dtype)


def _gather_rows(src, idx, n_rows_used, out_dtype):
    n, d = idx.shape[0], src.shape[1]
    g = GATHER_ROWS
    return pl.pallas_call(
        _gather_kernel,
        grid_spec=pltpu.PrefetchScalarGridSpec(
            num_scalar_prefetch=2, grid=(n // g,),
            in_specs=[pl.BlockSpec(memory_space=pl.ANY)],
            out_specs=pl.BlockSpec((g, d), lambda s, idx, ns: (s, 0)),
            scratch_shapes=[pltpu.VMEM((2, g, d), src.dtype), pltpu.SemaphoreType.DMA((2,))]),
        out_shape=jax.ShapeDtypeStruct((n, d), out_dtype),
        compiler_params=_params("arbitrary"),
        name="moe_gather",
    )(idx, n_rows_used // g, src)


def _tile_flags(te_ref, nu_ref):
    i = pl.program_id(1)
    changed = jnp.logical_or(i == 0, te_ref[i] != te_ref[jnp.maximum(i - 1, 0)])
    return changed, i < nu_ref[0]


def _moe_up_kernel(te_ref, tb_ref, nu_ref, x_ref, wg_ref, wu_ref, o_ref, wg_bf, wu_bf, *, d_exp):
    n = pl.program_id(0)
    tn = o_ref.shape[-1]
    changed, active = _tile_flags(te_ref, nu_ref)

    @pl.when(changed)
    def _():
        row = n * tn + lax.broadcasted_iota(jnp.int32, (tn, 1), 0)
        wg_bf[...] = jnp.where(row < d_exp, wg_ref[...], 0.0).astype(BF16)
        wu_bf[...] = jnp.where(row < d_exp, wu_ref[...], 0.0).astype(BF16)

    @pl.when(active)
    def _():
        x = x_ref[...]
        g = lax.dot_general(x, wg_bf[...], NT_DIMS, preferred_element_type=F32)
        u = lax.dot_general(x, wu_bf[...], NT_DIMS, preferred_element_type=F32)
        o_ref[...] = (_silu(g) * u).astype(BF16)

    @pl.when(jnp.logical_not(active))
    def _():
        o_ref[...] = jnp.zeros(o_ref.shape, BF16)


def _moe_up(xs, plan, w_gate_t, w_up_t, layer):
    _, _, tile_expert, tile_block, n_used = plan
    d = xs.shape[1]
    d_exp = w_gate_t.shape[2]
    n_tiles = tile_expert.shape[0]
    tn = DOWN_TK
    nt = pl.cdiv(d_exp, tn)
    wspec = pl.BlockSpec((None, None, tn, d), lambda n, i, te, tb, nu: (layer, te[i], n, 0))
    return pl.pallas_call(
        functools.partial(_moe_up_kernel, d_exp=d_exp),
        grid_spec=pltpu.PrefetchScalarGridSpec(
            num_scalar_prefetch=3, grid=(nt, n_tiles),
            in_specs=[pl.BlockSpec((MOE_TILE, d), lambda n, i, te, tb, nu: (tb[i], 0)), wspec, wspec],
            out_specs=pl.BlockSpec((MOE_TILE, tn), lambda n, i, te, tb, nu: (i, n)),
            scratch_shapes=[pltpu.VMEM((tn, d), BF16), pltpu.VMEM((tn, d), BF16)]),
        out_shape=jax.ShapeDtypeStruct((xs.shape[0], nt * tn), BF16),
        compiler_params=_params("arbitrary", "arbitrary"),
        name="moe_up",
    )(tile_expert, tile_block, n_used, xs, w_gate_t, w_up_t)


MOE_DOWN_TN = 1024


def _moe_down_kernel(te_ref, tb_ref, nu_ref, x_ref, w_ref, o_ref, w_bf):
    changed, active = _tile_flags(te_ref, nu_ref)
    k_valid = w_ref.shape[0]

    @pl.when(changed)
    def _():
        w_bf[:k_valid, :] = w_ref[...].astype(BF16)
        w_bf[k_valid:, :] = jnp.zeros((w_bf.shape[0] - k_valid, w_bf.shape[1]), BF16)

    @pl.when(active)
    def _():
        o_ref[...] = _bdot(x_ref[...], w_bf[...])

    @pl.when(jnp.logical_not(active))
    def _():
        o_ref[...] = jnp.zeros(o_ref.shape, F32)


def _moe_down(act, plan, w_down, layer):
    _, _, tile_expert, tile_block, n_used = plan
    k_pad = act.shape[1]
    k_valid, n = w_down.shape[2:]
    n_tiles = tile_expert.shape[0]
    tn = MOE_DOWN_TN
    return pl.pallas_call(
        _moe_down_kernel,
        grid_spec=pltpu.PrefetchScalarGridSpec(
            num_scalar_prefetch=3, grid=(n // tn, n_tiles),
            in_specs=[pl.BlockSpec((MOE_TILE, k_pad), lambda j, i, te, tb, nu: (tb[i], 0)),
                      pl.BlockSpec((None, None, k_valid, tn), lambda j, i, te, tb, nu: (layer, te[i], 0, j))],
            out_specs=pl.BlockSpec((MOE_TILE, tn), lambda j, i, te, tb, nu: (i, j)),
            scratch_shapes=[pltpu.VMEM((k_pad, tn), BF16)]),
        out_shape=jax.ShapeDtypeStruct((act.shape[0], n), F32),
        compiler_params=_params("arbitrary", "arbitrary"),
        name="moe_down",
    )(tile_expert, tile_block, n_used, act, w_down)


def _moe_combine_kernel(pos_ref, x_ref, wt_ref, gpost_ref, gate_ref, y_hbm, *rest, tok):
    xo_refs, (buf, sem) = rest[:-2], rest[-2:]
    tm = x_ref.shape[0]
    slot = _row_gather_pipeline(pos_ref, TOP_K * tm, y_hbm, buf, sem, pl.num_programs(0))
    wt = wt_ref[...]
    f = wt[:, 0:1] * buf[slot, pl.ds(0, tm), :]
    for k in range(1, TOP_K):
        f = f + wt[:, k:k + 1] * buf[slot, pl.ds(k * tm, tm), :]
    _store_rows(tok, xo_refs, x_ref[...] + gate_ref[...] * _rms(f, gpost_ref[...]))


def _moe_combine(tok, x, y, dest, wt, g_post, mod5, layer, seg_gate, split_out=False):
    t, d = x.shape
    tm = NORM_TILE
    pos = dest.reshape(t // tm, tm, TOP_K).transpose(0, 2, 1).reshape(-1)
    row = pl.BlockSpec((tm, d), lambda i, pos: (i, 0))
    return pl.pallas_call(
        functools.partial(_moe_combine_kernel, tok=tok),
        grid_spec=pltpu.PrefetchScalarGridSpec(
            num_scalar_prefetch=1, grid=(t // tm,),
            in_specs=[row, pl.BlockSpec((tm, TOP_K), lambda i, pos: (i, 0)),
                      pl.BlockSpec((None, 1, d), lambda i, pos: (layer, 0, 0)),
                      pl.BlockSpec((None, None, None, 1, d),
                                   lambda i, pos: (layer, tok.cond_row(i, tm), seg_gate, 0, 0)),
                      pl.BlockSpec(memory_space=pl.ANY)],
            out_specs=_split_row_specs(tok, d) if split_out else [row],
            scratch_shapes=[pltpu.VMEM((2, TOP_K * tm, d), F32), pltpu.SemaphoreType.DMA((2,))]),
        out_shape=_split_shapes(tok, d) if split_out else [jax.ShapeDtypeStruct((t, d), F32)],
        compiler_params=_params("arbitrary"),
        name="moe_combine",
    )(pos, x, wt, g_post.reshape(g_post.shape[0], 1, d), mod5, y)


SEG_SHIFT_MIX, SEG_SCALE_MIX, SEG_GATE_MIX, SEG_SHIFT_FFN, SEG_SCALE_FFN, SEG_GATE_FFN = range(N_MOD)


def kernel(x_prompt, x_sample, c, c_ctx, cache_k, cache_v, state_ssm_re, state_ssm_im, w_mod, b_mod, g_mix_pre, g_mix_post, g_ffn_pre, g_ffn_post, w_in, g_q, g_k, ssm_a_re, ssm_a_im, ssm_log_dt, ssm_b_re, ssm_b_im, ssm_c_re, ssm_c_im, ssm_d, w_glu, b_glu, conv_w, w_out, w_ffn_gate, w_ffn_up, w_ffn_down, w_router, b_router, w_exp_gate, w_exp_up, w_exp_down):
    n_ctx_seq, ctx_len, d = x_prompt.shape
    n_lat_seq, lat_len, _ = x_sample.shape
    depth = w_in.shape[0]
    tok = _Tokens(n_ctx_seq, ctx_len, n_lat_seq, lat_len)
    attn_w = d // 2
    kv_w = attn_w // Q_PER_KV
    ssm_w = ssm_d.shape[-1]
    n_kv = kv_w // HEAD_DIM
    n_dir, n_grp, n_state = ssm_a_re.shape[1:]
    nb = n_grp // SSM_GROUPS_PER_BLOCK
    sw = 2 * SSM_GROUPS_PER_BLOCK * n_state

    x = (x_prompt.reshape(tok.n_ctx, d), x_sample.reshape(tok.n_lat, d))
    cond = jnp.concatenate([c_ctx[None, :], c, jnp.zeros((8 - 1 - n_lat_seq, d), F32)], axis=0)
    mod = _modulation(cond, w_mod, b_mod)
    mod5 = mod.reshape(depth, cond.shape[0], N_MOD, 1, d)
    cos, sin = _rope_tables(tok)
    ck = cache_k.reshape(cache_k.shape[:3] + (kv_w,))
    cv = cache_v.reshape(cache_v.shape[:3] + (kv_w,))

    h = _prenorm(tok, x, g_mix_pre, mod5, 0, SEG_SHIFT_MIX, SEG_SCALE_MIX)
    new_k, new_v, new_re, new_im = [], [], [], []
    for l in range(depth):
        proj = _matmul(tok, [(h, 0)], w_in, l, F32)
        qn, kf, kb, vb = _qkv_prep(tok, proj, g_q, g_k, l, attn_w, kv_w, cos, sin)
        attn = (_attention_ctx(tok, qn, kb, vb), _attention_lat(tok, qn, kb, vb, ck, cv, l))
        new_k.append(kf[:tok.n_ctx].reshape(n_ctx_seq, ctx_len, n_kv, HEAD_DIM))
        new_v.append(proj[:tok.n_ctx, attn_w + kv_w:attn_w + 2 * kv_w].reshape(n_ctx_seq, ctx_len, n_kv, HEAD_DIM))

        u_col = attn_w + 2 * kv_w
        kt, e2, fc, a_q = _ssm_tables(ssm_a_re[l], ssm_a_im[l], ssm_log_dt[l], ssm_b_re[l], ssm_b_im[l],
                                      ssm_c_re[l], ssm_c_im[l])
        h0 = jnp.concatenate([state_ssm_re[:, l].reshape(n_lat_seq, n_dir, nb, sw // 2),
                              state_ssm_im[:, l].reshape(n_lat_seq, n_dir, nb, sw // 2)], axis=-1)
        ydir, fin = _ssm_scan(tok, proj, u_col // LANES, kt, e2, fc, a_q, h0.transpose(1, 2, 0, 3))
        fin = fin.transpose(2, 0, 1, 3).reshape(n_ctx_seq, n_dir, nb, 2, SSM_GROUPS_PER_BLOCK, n_state)
        new_re.append(fin[:, :, :, 0].reshape(n_ctx_seq, n_dir, n_grp, n_state))
        new_im.append(fin[:, :, :, 1].reshape(n_ctx_seq, n_dir, n_grp, n_state))
        ssm_out = _ssm_glu(proj, u_col // ssm_w, ydir, ssm_d, w_glu, b_glu, l)

        conv_out = _conv(tok, proj, (u_col + ssm_w) // ssm_w, conv_w, l)
        mix = _matmul(tok, [(attn, 0), (ssm_out, attn_w // ssm_w), (conv_out, attn_w // ssm_w + 1)],
                      w_out, l, F32)
        dense = l % 2 == 0
        x, h = _resid_norm(tok, x, mix, g_mix_post, mod5, l, SEG_GATE_MIX,
                           nxt=(g_ffn_pre, l, SEG_SHIFT_FFN, SEG_SCALE_FFN), h_dtype=BF16 if dense else F32)
        last = l + 1 == depth
        nxt = None if last else (g_mix_pre, l + 1, SEG_SHIFT_MIX, SEG_SCALE_MIX)
        if dense:
            act = _swiglu_up(h, w_ffn_gate, w_ffn_up, l // 2)
            f = _down(act, w_ffn_down.reshape((w_ffn_down.shape[0], 1) + w_ffn_down.shape[1:]), l // 2)
            outs = _resid_norm(tok, x, f, g_ffn_post, mod5, l, SEG_GATE_FFN, nxt=nxt, split_out=last)
            x, h = (tuple(outs), None) if last else outs
        else:
            m = l // 2
            idx, wt = _router(h, jnp.swapaxes(w_router, 1, 2), b_router, m)
            plan = _moe_plan(idx, w_router.shape[-1])
            xs = _gather_rows(h, plan[1], plan[4] * MOE_TILE, BF16)
            act = _moe_up(xs, plan, jnp.swapaxes(w_exp_gate, 2, 3), jnp.swapaxes(w_exp_up, 2, 3), m)
            y = _moe_down(act, plan, w_exp_down, m)
            outs = _moe_combine(tok, x, y, plan[0], wt, g_ffn_post, mod5, l, SEG_GATE_FFN, split_out=last)
            if last:
                x = tuple(outs)
            else:
                (x,) = outs
                h = _prenorm(tok, x, nxt[0], mod5, nxt[1], nxt[2], nxt[3])

    return (x[0].reshape(x_prompt.shape), x[1].reshape(x_sample.shape),
            jnp.stack(new_k, axis=1), jnp.stack(new_v, axis=1),
            jnp.stack(new_re, axis=1), jnp.stack(new_im, axis=1))
```

```python
import functools
import math

import jax
import jax.numpy as jnp
import numpy as np
from jax import lax
from jax.experimental import pallas as pl
from jax.experimental.pallas import tpu as pltpu

F32 = jnp.float32
BF16 = jnp.bfloat16

HEAD_DIM = 128
Q_PER_KV = 4
GRID_W = 64
SSM_GROUP = 16
N_MOD = 6
TOP_K = 2
ROPE_THETA = 10000.0
EPS = 1e-6

LANES = 128
MXU_DIM = 256
VMEM_LIMIT_BYTES = 56 * 1024 * 1024

SSM_CHUNK = 8
SSM_SCAN_UNROLL = 4
SSM_GROUPS_PER_BLOCK = LANES // SSM_GROUP


def _params(*semantics):
    return pltpu.CompilerParams(dimension_semantics=semantics, vmem_limit_bytes=VMEM_LIMIT_BYTES)


def _rms(x, g):
    return x * lax.rsqrt(jnp.mean(x * x, axis=-1, keepdims=True) + EPS) * g


def _silu(x):
    return x * jax.nn.sigmoid(x)


def _bdot(a, b):
    return jnp.dot(a, b, preferred_element_type=F32)


class _Tokens:
    def __init__(self, n_ctx_seq, ctx_len, n_lat_seq, lat_len):
        self.n_ctx_seq, self.ctx_len = n_ctx_seq, ctx_len
        self.n_lat_seq, self.lat_len = n_lat_seq, lat_len
        self.n_ctx = n_ctx_seq * ctx_len
        self.n_lat = n_lat_seq * lat_len
        self.total = self.n_ctx + self.n_lat

    def cond_row(self, i, tile):
        ctx_tiles = self.n_ctx // tile
        lat_tiles = self.lat_len // tile
        return jnp.where(i < ctx_tiles, 0, 1 + (i - ctx_tiles) // lat_tiles)


def _mod_kernel(c_ref, w_ref, b_ref, o_ref):
    s = _silu(c_ref[...]).astype(BF16)
    o_ref[...] = _bdot(s, w_ref[...].astype(BF16)) + b_ref[...]


def _modulation(cond, w_mod, b_mod):
    depth, d, n = w_mod.shape
    rows = cond.shape[0]
    tn = 512
    return pl.pallas_call(
        _mod_kernel,
        grid=(depth, n // tn),
        in_specs=[
            pl.BlockSpec((rows, d), lambda l, j: (0, 0)),
            pl.BlockSpec((None, d, tn), lambda l, j: (l, 0, j)),
            pl.BlockSpec((None, 1, tn), lambda l, j: (l, 0, j)),
        ],
        out_specs=pl.BlockSpec((None, rows, tn), lambda l, j: (l, 0, j)),
        out_shape=jax.ShapeDtypeStruct((depth, rows, n), F32),
        compiler_params=_params("parallel", "parallel"),
        name="modulation",
    )(cond, w_mod, b_mod.reshape(depth, 1, n))


NORM_TILE = 256


def _mod_spec(tok, layer, seg, d):
    return pl.BlockSpec((None, None, None, 1, d),
                        lambda i: (layer, tok.cond_row(i, NORM_TILE), seg, 0, 0))


def _split_row_specs(tok, d):
    ct = tok.n_ctx // NORM_TILE
    return [pl.BlockSpec((NORM_TILE, d), lambda i, *_: (jnp.minimum(i, ct - 1), 0)),
            pl.BlockSpec((NORM_TILE, d), lambda i, *_: (jnp.maximum(i - ct, 0), 0))]


def _split_shapes(tok, d):
    return [jax.ShapeDtypeStruct((tok.n_ctx, d), F32), jax.ShapeDtypeStruct((tok.n_lat, d), F32)]


def _load_rows(tok, refs):
    if len(refs) == 1:
        return refs[0][...]
    return jnp.where(pl.program_id(0) < tok.n_ctx // NORM_TILE, refs[0][...], refs[1][...])


def _store_rows(tok, refs, val):
    if len(refs) == 1:
        refs[0][...] = val
        return
    is_ctx = pl.program_id(0) < tok.n_ctx // NORM_TILE

    @pl.when(is_ctx)
    def _():
        refs[0][...] = val

    @pl.when(jnp.logical_not(is_ctx))
    def _():
        refs[1][...] = val


def _as_rows(x):
    return list(x) if isinstance(x, (tuple, list)) else [x]


def _prenorm_kernel(*refs, tok, n_x):
    g_ref, sc_ref, sh_ref, h_ref = refs[n_x:]
    h = _rms(_load_rows(tok, refs[:n_x]), g_ref[...]) * (1.0 + sc_ref[...]) + sh_ref[...]
    h_ref[...] = h.astype(BF16)


def _prenorm(tok, x, g, mod5, layer, seg_shift, seg_scale):
    xs = _as_rows(x)
    d = xs[0].shape[1]
    row = pl.BlockSpec((NORM_TILE, d), lambda i: (i, 0))
    return pl.pallas_call(
        functools.partial(_prenorm_kernel, tok=tok, n_x=len(xs)),
        grid=(tok.total // NORM_TILE,),
        in_specs=([row] if len(xs) == 1 else _split_row_specs(tok, d))
        + [pl.BlockSpec((None, 1, d), lambda i: (layer, 0, 0)),
           _mod_spec(tok, layer, seg_scale, d), _mod_spec(tok, layer, seg_shift, d)],
        out_specs=row,
        out_shape=jax.ShapeDtypeStruct((tok.total, d), BF16),
        compiler_params=_params("arbitrary"),
        name="prenorm",
    )(*xs, g.reshape(g.shape[0], 1, d), mod5, mod5)


def _resid_norm_kernel(*refs, tok, n_x, n_xo, with_next, with_router):
    y_ref, gpost_ref, gate_ref = refs[n_x:n_x + 3]
    rest = list(refs[n_x + 3:])
    x = _load_rows(tok, refs[:n_x]) + gate_ref[...] * _rms(y_ref[...], gpost_ref[...])
    if with_next:
        gpre_ref, sc_ref, sh_ref = rest[:3]
        rest = rest[3:]
        h = _rms(x, gpre_ref[...]) * (1.0 + sc_ref[...]) + sh_ref[...]
        if with_router:
            wr_ref, br_ref = rest[:2]
            rest = rest[2:]
            idx_ref, wt_ref = rest[n_xo + 1:]
            idx_ref[...], wt_ref[...] = _route(h, wr_ref[...], br_ref[...])
        h_ref = rest[n_xo]
        h_ref[...] = h.astype(h_ref.dtype)
    _store_rows(tok, rest[:n_xo], x)


def _resid_norm(tok, x, y, g_post, mod5, layer, seg_gate, nxt=None, h_dtype=BF16, split_out=False, router=None):
    xs = _as_rows(x)
    t, d = y.shape
    row = pl.BlockSpec((NORM_TILE, d), lambda i: (i, 0))
    in_specs = ([row] if len(xs) == 1 else _split_row_specs(tok, d)) + [
        row, pl.BlockSpec((None, 1, d), lambda i: (layer, 0, 0)), _mod_spec(tok, layer, seg_gate, d)]
    args = xs + [y, g_post.reshape(g_post.shape[0], 1, d), mod5]
    out_specs = _split_row_specs(tok, d) if split_out else [row]
    out_shape = _split_shapes(tok, d) if split_out else [jax.ShapeDtypeStruct((t, d), F32)]
    n_xo = len(out_specs)
    if nxt is not None:
        g_pre, nl, seg_shift, seg_scale = nxt
        in_specs += [pl.BlockSpec((None, 1, d), lambda i: (nl, 0, 0)),
                     _mod_spec(tok, nl, seg_scale, d), _mod_spec(tok, nl, seg_shift, d)]
        args += [g_pre.reshape(g_pre.shape[0], 1, d), mod5, mod5]
        out_specs.append(row)
        out_shape.append(jax.ShapeDtypeStruct((t, d), h_dtype))
    if router is not None:
        w_router_t, b_router, rl = router
        n_exp = w_router_t.shape[1]
        in_specs += [pl.BlockSpec((None, n_exp, d), lambda i: (rl, 0, 0)),
                     pl.BlockSpec((None, 1, n_exp), lambda i: (rl, 0, 0))]
        args += [w_router_t, b_router.reshape(-1, 1, n_exp)]
        out_specs += [pl.BlockSpec((NORM_TILE, TOP_K), lambda i: (i, 0))] * 2
        out_shape += [jax.ShapeDtypeStruct((t, TOP_K), jnp.int32), jax.ShapeDtypeStruct((t, TOP_K), F32)]
    return pl.pallas_call(
        functools.partial(_resid_norm_kernel, tok=tok, n_x=len(xs), n_xo=n_xo, with_next=nxt is not None,
                          with_router=router is not None),
        grid=(t // NORM_TILE,),
        in_specs=in_specs, out_specs=out_specs, out_shape=out_shape,
        compiler_params=_params("arbitrary"),
        name="resid_norm",
    )(*args)


MM_TM = 1024
MM_TN = 512


def _mm_kernel(*refs, n_parts, ctx_tiles):
    o_ref = refs[-1]
    n_x = sum(n_parts)
    acc, at = None, 0
    for s, parts in enumerate(n_parts):
        x = refs[at][...]
        if parts == 2:
            x = jnp.where(pl.program_id(0) < ctx_tiles, x, refs[at + 1][...])
        at += parts
        p = _bdot(x, refs[n_x + s][...].astype(BF16))
        acc = p if acc is None else acc + p
    o_ref[...] = acc.astype(o_ref.dtype)


def _matmul(tok, segments, w, layer, out_dtype, tn=MM_TN):
    m = tok.total
    n = w.shape[-1]
    ct = tok.n_ctx // MM_TM
    in_specs, args, n_parts = [], [], []
    for x, _ in segments:
        xs = _as_rows(x)
        k = xs[0].shape[1]
        if len(xs) == 1:
            in_specs.append(pl.BlockSpec((MM_TM, k), lambda i, j: (i, 0)))
        else:
            in_specs += [pl.BlockSpec((MM_TM, k), lambda i, j: (jnp.minimum(i, ct - 1), 0)),
                         pl.BlockSpec((MM_TM, k), lambda i, j: (jnp.maximum(i - ct, 0), 0))]
        args += xs
        n_parts.append(len(xs))
    for x, rb in segments:
        k = _as_rows(x)[0].shape[1]
        in_specs.append(pl.BlockSpec((None, k, tn), lambda i, j, rb=rb: (layer, rb, j)))
        args.append(w)
    return pl.pallas_call(
        functools.partial(_mm_kernel, n_parts=tuple(n_parts), ctx_tiles=ct),
        grid=(m // MM_TM, n // tn),
        in_specs=in_specs,
        out_specs=pl.BlockSpec((MM_TM, tn), lambda i, j: (i, j)),
        out_shape=jax.ShapeDtypeStruct((m, n), out_dtype),
        compiler_params=_params("parallel", "parallel"),
        name="matmul",
    )(*args)


def _swiglu_up_kernel(x_ref, wg_ref, wu_ref, o_ref):
    x = x_ref[...]
    g = _bdot(x, wg_ref[...].astype(BF16))
    u = _bdot(x, wu_ref[...].astype(BF16))
    o_ref[...] = (_silu(g) * u).astype(BF16)


def _swiglu_up(h, w_gate, w_up, layer):
    m, d = h.shape
    n = w_gate.shape[-1]
    tn = 256
    wspec = pl.BlockSpec((None, d, tn), lambda i, j: (layer, 0, j))
    return pl.pallas_call(
        _swiglu_up_kernel,
        grid=(m // MM_TM, n // tn),
        in_specs=[pl.BlockSpec((MM_TM, d), lambda i, j: (i, 0)), wspec, wspec],
        out_specs=pl.BlockSpec((MM_TM, tn), lambda i, j: (i, j)),
        out_shape=jax.ShapeDtypeStruct((m, n), BF16),
        compiler_params=_params("parallel", "parallel"),
        name="swiglu_up",
    )(h, w_gate, w_up)


DOWN_TK = 256


def _down_kernel(x_ref, w_ref, o_ref, *, k_valid):
    k = pl.program_id(2)

    @pl.when(jnp.logical_and(pl.program_id(1) == 0, k == 0))
    def _():
        o_ref[...] = jnp.zeros(o_ref.shape, F32)

    w = w_ref[...]
    if k_valid % DOWN_TK:
        row = k * DOWN_TK + lax.broadcasted_iota(jnp.int32, (DOWN_TK, 1), 0)
        w = jnp.where(row < k_valid, w, 0.0)
    w = w.astype(BF16)
    x = x_ref[...]
    for c in range(0, o_ref.shape[1], MM_TN):
        o_ref[:, c:c + MM_TN] += _bdot(x, w[:, c:c + MM_TN])


def _down(act, w, layer):
    m = act.shape[0]
    n_grp, k_valid, n = w.shape[-3:]
    kt = pl.cdiv(k_valid, DOWN_TK)
    assert act.shape[1] == n_grp * kt * DOWN_TK
    return pl.pallas_call(
        functools.partial(_down_kernel, k_valid=k_valid),
        grid=(m // MM_TM, n_grp, kt),
        in_specs=[pl.BlockSpec((MM_TM, DOWN_TK), lambda i, g, k: (i, g * kt + k)),
                  pl.BlockSpec((None, None, DOWN_TK, n), lambda i, g, k: (layer, g, k, 0))],
        out_specs=pl.BlockSpec((MM_TM, n), lambda i, g, k: (i, 0)),
        out_shape=jax.ShapeDtypeStruct((m, n), F32),
        compiler_params=_params("parallel", "arbitrary", "arbitrary"),
        name="down",
    )(act, w)


ATTN_TILE = 256
CTX_HEADS_PER_CHAIN = 2
LAT_HEADS_PER_CHAIN = 1


def _rope_tables(tok):
    pos = np.arange(tok.lat_len)
    half = HEAD_DIM // 2
    inv = ROPE_THETA ** (-np.arange(0, half, 2, dtype=np.float64) / half)
    ang_r = (pos // GRID_W)[:, None] * inv[None, :]
    ang_c = (pos % GRID_W)[:, None] * inv[None, :]
    cos = np.concatenate([np.cos(ang_r)] * 2 + [np.cos(ang_c)] * 2, axis=-1)
    sin = np.concatenate([-np.sin(ang_r), np.sin(ang_r), -np.sin(ang_c), np.sin(ang_c)], axis=-1)
    ident = np.ones((ATTN_TILE, HEAD_DIM), np.float32)
    return (jnp.asarray(np.concatenate([ident, cos], axis=0), F32),
            jnp.asarray(np.concatenate([0.0 * ident, sin], axis=0), F32))


def _qkv_prep_kernel(q_ref, k_ref, v_ref, gq_ref, gk_ref, cos_ref, sin_ref,
                     qo_ref, kf_ref, kb_ref, vb_ref, *, ctx_tiles):
    def run(rope):
        if rope:
            cos, sin = cos_ref[...], sin_ref[...]
            lane = lax.broadcasted_iota(jnp.int32, cos.shape, 1)
            first = (lane % (HEAD_DIM // 2)) < (HEAD_DIM // 4)
            quarter = HEAD_DIM // 4

        def norm_rope(x, g):
            xn = _rms(x, g)
            if not rope:
                return xn
            partner = jnp.where(first, pltpu.roll(xn, HEAD_DIM - quarter, 1), pltpu.roll(xn, quarter, 1))
            return xn * cos + partner * sin

        gq, gk = gq_ref[...], gk_ref[...]
        for h in range(q_ref.shape[1] // HEAD_DIM):
            sl = slice(h * HEAD_DIM, (h + 1) * HEAD_DIM)
            qo_ref[:, sl] = norm_rope(q_ref[:, sl], gq).astype(BF16)
        for h in range(k_ref.shape[1] // HEAD_DIM):
            sl = slice(h * HEAD_DIM, (h + 1) * HEAD_DIM)
            kn = norm_rope(k_ref[:, sl], gk)
            kf_ref[:, sl] = kn
            kb_ref[:, sl] = kn.astype(BF16)

    is_ctx = pl.program_id(0) < ctx_tiles
    pl.when(is_ctx)(functools.partial(run, False))
    pl.when(jnp.logical_not(is_ctx))(functools.partial(run, True))
    vb_ref[...] = v_ref[...].astype(BF16)


def _qkv_prep(tok, proj, g_q, g_k, layer, attn_w, kv_w, cos, sin):
    t = proj.shape[0]
    tm = ATTN_TILE
    ctx_tiles = tok.n_ctx // tm
    lat_tiles = tok.lat_len // tm

    def table_row(i):
        return jnp.where(i < ctx_tiles, 0, 1 + (i - ctx_tiles) % lat_tiles)

    kv_blk = attn_w // kv_w
    gspec = pl.BlockSpec((None, 1, HEAD_DIM), lambda i: (layer, 0, 0))
    tspec = pl.BlockSpec((tm, HEAD_DIM), lambda i: (table_row(i), 0))
    kvspec = pl.BlockSpec((tm, kv_w), lambda i: (i, 0))
    return pl.pallas_call(
        functools.partial(_qkv_prep_kernel, ctx_tiles=ctx_tiles),
        grid=(t // tm,),
        in_specs=[pl.BlockSpec((tm, attn_w), lambda i: (i, 0)),
                  pl.BlockSpec((tm, kv_w), lambda i: (i, kv_blk)),
                  pl.BlockSpec((tm, kv_w), lambda i: (i, kv_blk + 1)),
                  gspec, gspec, tspec, tspec],
        out_specs=[pl.BlockSpec((tm, attn_w), lambda i: (i, 0)), kvspec, kvspec, kvspec],
        out_shape=[jax.ShapeDtypeStruct((t, attn_w), BF16), jax.ShapeDtypeStruct((t, kv_w), F32),
                   jax.ShapeDtypeStruct((t, kv_w), BF16), jax.ShapeDtypeStruct((t, kv_w), BF16)],
        compiler_params=_params("parallel"),
        name="qkv_prep",
    )(proj, proj, proj, g_q.reshape(-1, 1, HEAD_DIM), g_k.reshape(-1, 1, HEAD_DIM), cos, sin)


def _attn_kernel(q_ref, *refs, n_kv, heads_per_chain):
    o_ref = refs[-1]
    tq = q_ref.shape[0]
    ks = [refs[2 * p][...].astype(BF16) for p in range(n_kv)]
    vs = [refs[2 * p + 1][...].astype(BF16) for p in range(n_kv)]
    scale_log2e = HEAD_DIM ** -0.5 * math.log2(math.e)
    for g0 in range(0, Q_PER_KV, heads_per_chain):
        heads = range(g0, g0 + heads_per_chain)
        q = jnp.concatenate([q_ref[:, g * HEAD_DIM:(g + 1) * HEAD_DIM] for g in heads], axis=0)
        ss = [lax.dot_general(q, k, (((1,), (1,)), ((), ())), preferred_element_type=F32) for k in ks]
        m = functools.reduce(jnp.maximum, [jnp.max(s, axis=-1, keepdims=True) for s in ss])
        ps = [jnp.exp2((s - m) * scale_log2e) for s in ss]
        denom = functools.reduce(jnp.add, [jnp.sum(p, axis=-1, keepdims=True) for p in ps])
        o = functools.reduce(jnp.add, [_bdot(p.astype(BF16), v) for p, v in zip(ps, vs)])
        o = o / denom
        for n, g in enumerate(heads):
            o_ref[:, g * HEAD_DIM:(g + 1) * HEAD_DIM] = o[n * tq:(n + 1) * tq].astype(BF16)


def _attention_ctx(tok, qn, kb, vb):
    L = tok.ctx_len
    n_kv = kb.shape[1] // HEAD_DIM
    qw = Q_PER_KV * HEAD_DIM
    kvspec = pl.BlockSpec((L, HEAD_DIM), lambda s, h: (s, h))
    qspec = pl.BlockSpec((L, qw), lambda s, h: (s, h))
    return pl.pallas_call(
        functools.partial(_attn_kernel, n_kv=1, heads_per_chain=CTX_HEADS_PER_CHAIN),
        grid=(tok.n_ctx_seq, n_kv),
        in_specs=[qspec, kvspec, kvspec],
        out_specs=qspec,
        out_shape=jax.ShapeDtypeStruct((tok.n_ctx, qn.shape[1]), BF16),
        compiler_params=_params("parallel", "parallel"),
        name="attn_ctx",
    )(qn, kb, vb)


def _attention_lat(tok, qn, kb, vb, cache_k, cache_v, layer):
    L = tok.lat_len
    n_kv = kb.shape[1] // HEAD_DIM
    qw = Q_PER_KV * HEAD_DIM
    tq = ATTN_TILE
    q0 = tok.n_ctx // tq
    k0 = tok.n_ctx // L
    past = cache_k.shape[2]
    cspec = pl.BlockSpec((None, None, past, HEAD_DIM), lambda b, h, i: (b, layer, 0, h))
    kvspec = pl.BlockSpec((L, HEAD_DIM), lambda b, h, i: (k0 + b, h))
    return pl.pallas_call(
        functools.partial(_attn_kernel, n_kv=2, heads_per_chain=LAT_HEADS_PER_CHAIN),
        grid=(tok.n_lat_seq, n_kv, L // tq),
        in_specs=[pl.BlockSpec((tq, qw), lambda b, h, i: (q0 + b * (L // tq) + i, h)),
                  cspec, cspec, kvspec, kvspec],
        out_specs=pl.BlockSpec((tq, qw), lambda b, h, i: (b * (L // tq) + i, h)),
        out_shape=jax.ShapeDtypeStruct((tok.n_lat, qn.shape[1]), BF16),
        compiler_params=_params("parallel", "parallel", "parallel"),
        name="attn_lat",
    )(qn, cache_k, cache_v, kb, vb)


def _ssm_tables(a_re, a_im, log_dt, b_re, b_im, c_re, c_im):
    hi = lax.Precision.HIGHEST
    n_dir, G, P = a_re.shape
    Q, gb = SSM_CHUNK, SSM_GROUPS_PER_BLOCK
    nb = G // gb
    dt = jnp.exp(log_dt)[None, :, :, None]
    ks = jnp.arange(Q + 1, dtype=F32)[:, None, None, None]
    mag = jnp.exp(ks * dt * a_re[None])
    ang = ks * dt * a_im[None]
    pr, pi = mag * jnp.cos(ang), mag * jnp.sin(ang)
    nr, ni = pr[1] - 1.0, pi[1]
    den = a_re * a_re + a_im * a_im
    qr, qi = (nr * a_re + ni * a_im) / den, (ni * a_re - nr * a_im) / den
    bt_re, bt_im = jnp.swapaxes(b_re, -1, -2), jnp.swapaxes(b_im, -1, -2)
    bbr = qr[:, :, None, :] * bt_re - qi[:, :, None, :] * bt_im
    bbi = qr[:, :, None, :] * bt_im + qi[:, :, None, :] * bt_re
    pkr, pki = pr[:Q, :, :, None, :], pi[:Q, :, :, None, :]
    pbr = pkr * bbr[None] - pki * bbi[None]
    pbi = pkr * bbi[None] + pki * bbr[None]
    pcr, pci = pr[1:, :, :, None, :], pi[1:, :, :, None, :]
    car = c_re[None] * pcr - c_im[None] * pci
    cai = c_re[None] * pci + c_im[None] * pcr

    def rows128(re, im):
        t = jnp.concatenate([re, im], axis=-1)
        return t.reshape(t.shape[:-3] + (nb, LANES, 2 * P))

    def lead(t):
        return jnp.transpose(t, (1, 2, 0) + tuple(range(3, t.ndim)))

    e1 = rows128(pbr, pbi)
    kt = jnp.einsum('kdjan,djbn->kdjab', e1, rows128(c_re, -c_im), precision=hi)
    a_q = jnp.concatenate([pr[Q].reshape(n_dir, nb, 1, gb * P), pi[Q].reshape(n_dir, nb, 1, gb * P)], axis=-1)
    return lead(kt), lead(e1), lead(rows128(car, -cai)), a_q


def _ssm_kernel(u_ref, kt_ref, e1_ref, ca_ref, aq_ref, h0_ref, y_ref, fin_ref,
                we_scr, wy_scr, s_scr, hin_scr, *, tok):
    Q = SSM_CHUNK
    rows = u_ref.shape[0] // Q
    nlb = s_scr.shape[0]
    half = nlb // 2
    sw = nlb * LANES
    n_state = e1_ref.shape[2] // 2
    reverse = pl.program_id(0) == 1

    def lanes(b):
        return slice(b * LANES, (b + 1) * LANES)

    def iota(shape, dim):
        return lax.broadcasted_iota(jnp.int32, shape, dim)

    spread = jnp.where(jnp.logical_and(iota((LANES, sw), 0) // n_state == iota((LANES, sw), 1) // (sw // 2),
                                       iota((LANES, sw), 0) % n_state == iota((LANES, sw), 1) % n_state),
                       1.0, 0.0).astype(BF16)
    own_state = iota((LANES, sw), 0) // SSM_GROUP == (iota((LANES, sw), 1) % (sw // 2)) // n_state
    for s in range(Q):
        e1 = e1_ref[jnp.where(reverse, s, Q - 1 - s)].astype(BF16)
        we_scr[lanes(s), :] = jnp.where(own_state, _bdot(e1, spread), 0.0).astype(BF16)
    own_group = iota((LANES, LANES), 0) // SSM_GROUP == iota((LANES, LANES), 1) // SSM_GROUP
    for s in range(Q):
        for t in range(Q):
            if s == t:
                keep = own_group
            elif t > s:
                keep = jnp.logical_and(own_group, jnp.logical_not(reverse))
            else:
                keep = jnp.logical_and(own_group, reverse)
            wy_scr[lanes(s), lanes(t)] = jnp.where(keep, kt_ref[abs(t - s)], 0.0).astype(BF16)
    own_state_t = (iota((sw, LANES), 0) % (sw // 2)) // n_state == iota((sw, LANES), 1) // SSM_GROUP
    n_rep = sw // 2 // n_state
    for t in range(Q):
        fc = ca_ref[jnp.where(reverse, Q - 1 - t, t)].T
        tiled = jnp.concatenate([fc[:n_state]] * n_rep + [fc[n_state:]] * n_rep, axis=0)
        wy_scr[Q * LANES:, lanes(t)] = jnp.where(own_state_t, tiled, 0.0).astype(BF16)

    x = jnp.concatenate([u_ref[pl.ds(t, rows, stride=Q), :].astype(BF16) for t in range(Q)], axis=1)
    s = _bdot(x, we_scr[...])
    for b in range(nlb):
        s_scr[b] = s[:, lanes(b)]
    a = [aq_ref[:, lanes(b)] for b in range(nlb)]

    def scan(n_seq, n_chunk, row0, h_init):
        def step(i, h):
            c = jnp.where(reverse, n_chunk - 1 - i, i)
            sel = pl.ds(row0 + c, n_seq, stride=n_chunk)
            new = [None] * nlb
            for b in range(half):
                hr, hi = h[b], h[half + b]
                hin_scr[b, sel, :] = hr
                hin_scr[half + b, sel, :] = hi
                new[b] = a[b] * hr - a[half + b] * hi + s_scr[b, sel, :]
                new[half + b] = a[b] * hi + a[half + b] * hr + s_scr[half + b, sel, :]
            return tuple(new)
        return lax.fori_loop(0, n_chunk, step, h_init, unroll=SSM_SCAN_UNROLL)

    n_ctx_rows = tok.n_ctx // Q
    zeros = tuple(jnp.zeros((tok.n_ctx_seq, LANES), F32) for _ in range(nlb))
    fin = scan(tok.n_ctx_seq, tok.ctx_len // Q, 0, zeros)
    for b in range(nlb):
        fin_ref[:, lanes(b)] = fin[b]
    scan(tok.n_lat_seq, tok.lat_len // Q, n_ctx_rows, tuple(h0_ref[:, lanes(b)] for b in range(nlb)))
    xin = jnp.concatenate([x] + [hin_scr[b].astype(BF16) for b in range(nlb)], axis=1)
    y = _bdot(xin, wy_scr[...])
    for t in range(Q):
        y_ref[pl.ds(t, rows, stride=Q), :] = y[:, lanes(t)]


def _ssm_scan(tok, proj, col_block0, kt, e2, fc, a_q, h0):
    n_dir, nb = kt.shape[:2]
    t = proj.shape[0]
    Q = SSM_CHUNK
    rows = t // Q
    sw = a_q.shape[-1]

    def table(x):
        return pl.BlockSpec((None, None) + x.shape[2:], lambda d, j: (d, j) + (0,) * (x.ndim - 2))

    return pl.pallas_call(
        functools.partial(_ssm_kernel, tok=tok),
        grid=(n_dir, nb),
        in_specs=[pl.BlockSpec((t, LANES), lambda d, j: (0, col_block0 + j)),
                  table(kt), table(e2), table(fc), table(a_q), table(h0)],
        out_specs=[pl.BlockSpec((None, t, LANES), lambda d, j: (d, 0, j)),
                   pl.BlockSpec((None, None, tok.n_ctx_seq, sw), lambda d, j: (d, j, 0, 0))],
        out_shape=[jax.ShapeDtypeStruct((n_dir, t, nb * LANES), F32),
                   jax.ShapeDtypeStruct((n_dir, nb, tok.n_ctx_seq, sw), F32)],
        scratch_shapes=[pltpu.VMEM((Q * LANES, sw), BF16), pltpu.VMEM((Q * LANES + sw, Q * LANES), BF16),
                        pltpu.VMEM((sw // LANES, rows, LANES), F32), pltpu.VMEM((sw // LANES, rows, LANES), F32)],
        compiler_params=_params("arbitrary", "arbitrary"),
        name="ssm_scan",
    )(proj, kt, e2, fc, a_q, h0)


def _ssm_glu_kernel(u_ref, yf_ref, yb_ref, d_ref, w_ref, b_ref, o_ref):
    y = d_ref[...] * u_ref[...]
    y = y + yf_ref[...]
    y = y + yb_ref[...]
    z = jax.nn.gelu(y)
    gate = jax.nn.sigmoid(_bdot(z.astype(BF16), w_ref[...].astype(BF16)) + b_ref[...])
    o_ref[...] = (z * gate).astype(BF16)


def _ssm_glu(proj, col_block, ydir, ssm_d, w_glu, b_glu, layer):
    t = proj.shape[0]
    w = ydir.shape[-1]
    tm = 512
    row = pl.BlockSpec((None, 1, w), lambda i: (layer, 0, 0))
    return pl.pallas_call(
        _ssm_glu_kernel,
        grid=(t // tm,),
        in_specs=[pl.BlockSpec((tm, w), lambda i: (i, col_block)),
                  pl.BlockSpec((None, tm, w), lambda i: (0, i, 0)),
                  pl.BlockSpec((None, tm, w), lambda i: (1, i, 0)),
                  row, pl.BlockSpec((None, w, w), lambda i: (layer, 0, 0)), row],
        out_specs=pl.BlockSpec((tm, w), lambda i: (i, 0)),
        out_shape=jax.ShapeDtypeStruct((t, w), BF16),
        compiler_params=_params("parallel"),
        name="ssm_glu",
    )(proj, ydir, ydir, ssm_d.reshape(-1, 1, w), w_glu, b_glu.reshape(-1, 1, w))


CONV_TILE = 256
CONV_HALO = 8


def _conv_kernel(cx_ref, cb_ref, cc_ref, pcx_ref, pcc_ref, ncx_ref, ncc_ref, w_ref, o_ref, *, tok):
    i = pl.program_id(0)
    tm = CONV_TILE
    tiles_per_lat = tok.lat_len // tm
    ctx_tiles = tok.n_ctx // tm
    j = (i - ctx_tiles) % tiles_per_lat
    is_ctx = i < ctx_tiles
    has_prev = jnp.logical_and(jnp.logical_not(is_ctx), j > 0)
    has_next = jnp.logical_and(jnp.logical_not(is_ctx), j < tiles_per_lat - 1)
    z = cc_ref[...] * cx_ref[...]
    z_prev = jnp.where(has_prev, pcc_ref[CONV_HALO - 1:, :] * pcx_ref[CONV_HALO - 1:, :], 0.0)
    z_next = jnp.where(has_next, ncc_ref[:1, :] * ncx_ref[:1, :], 0.0)
    row = lax.broadcasted_iota(jnp.int32, z.shape, 0)
    up = jnp.where(row == 0, z_prev, pltpu.roll(z, 1, 0))
    dn = jnp.where(row == tm - 1, z_next, pltpu.roll(z, tm - 1, 0))
    y = up * w_ref[0:1, :] + z * w_ref[1:2, :] + dn * w_ref[2:3, :]
    o_ref[...] = (cb_ref[...] * y).astype(BF16)


def _conv(tok, proj, col_block0, conv_w, layer):
    t = proj.shape[0]
    w = conv_w.shape[-1]
    tm = CONV_TILE
    assert tok.ctx_len == tm and tok.lat_len % tm == 0
    hb = tm // CONV_HALO
    last = t // CONV_HALO - 1

    def main(c):
        return pl.BlockSpec((tm, w), lambda i: (i, col_block0 + c))

    def prev(c):
        return pl.BlockSpec((CONV_HALO, w), lambda i: (jnp.maximum(i * hb - 1, 0), col_block0 + c))

    def nxt(c):
        return pl.BlockSpec((CONV_HALO, w), lambda i: (jnp.minimum((i + 1) * hb, last), col_block0 + c))

    return pl.pallas_call(
        functools.partial(_conv_kernel, tok=tok),
        grid=(t // tm,),
        in_specs=[main(0), main(1), main(2), prev(0), prev(2), nxt(0), nxt(2),
                  pl.BlockSpec((None,) + conv_w.shape[1:], lambda i: (layer, 0, 0))],
        out_specs=pl.BlockSpec((tm, w), lambda i: (i, 0)),
        out_shape=jax.ShapeDtypeStruct((t, w), BF16),
        compiler_params=_params("parallel"),
        name="conv",
    )(proj, proj, proj, proj, proj, proj, proj, conv_w)


MOE_TILE = 512
MOE_UP_TN = 512
GATHER_ROWS = 256
N_DMA_QUEUES = 2
NT_DIMS = (((1,), (1,)), ((), ()))


def _route(h, w, b):
    hh, wh = h.astype(BF16), w.astype(BF16)
    hl, wl = (h - hh.astype(F32)).astype(BF16), (w - wh.astype(F32)).astype(BF16)

    def nt(x, y):
        return lax.dot_general(x, y, NT_DIMS, preferred_element_type=F32)

    lg = nt(hh, wh) + nt(hl, wh) + nt(hh, wl) + b
    n_exp = float(lg.shape[-1])
    idx = lax.broadcasted_iota(jnp.int32, lg.shape, 1).astype(F32)
    m1 = jnp.max(lg, axis=-1, keepdims=True)
    i1 = jnp.min(jnp.where(lg == m1, idx, n_exp), axis=-1, keepdims=True)
    rest = jnp.where(idx == i1, -jnp.inf, lg)
    m2 = jnp.max(rest, axis=-1, keepdims=True)
    i2 = jnp.min(jnp.where(rest == m2, idx, n_exp), axis=-1, keepdims=True)
    e = jnp.exp(m2 - m1)
    w1 = 1.0 / (1.0 + e)
    slot = lax.broadcasted_iota(jnp.int32, (h.shape[0], TOP_K), 1)
    return jnp.where(slot == 0, i1, i2).astype(jnp.int32), jnp.where(slot == 0, w1, e * w1)


def _moe_plan(idx, n_exp):
    t, k = idx.shape
    e = idx.reshape(-1)
    onehot = (e[:, None] == jnp.arange(n_exp, dtype=jnp.int32)[None, :]).astype(jnp.int32)
    csum = jnp.cumsum(onehot, axis=0)
    counts = csum[-1]
    rank = jnp.take_along_axis(csum, e[:, None], axis=1)[:, 0] - 1
    tiles_e = (counts + MOE_TILE - 1) // MOE_TILE
    tile_end = jnp.cumsum(tiles_e)
    tile_start = tile_end - tiles_e
    n_used = tile_end[-1]
    dest = tile_start[e] * MOE_TILE + rank
    n_tiles = (t * k) // MOE_TILE + n_exp
    ti = jnp.arange(n_tiles, dtype=jnp.int32)
    tile_block = jnp.minimum(ti, n_used - 1)
    tile_expert = jnp.sum((tile_end[None, :] <= tile_block[:, None]).astype(jnp.int32), axis=1)
    tile_expert = jnp.minimum(tile_expert, n_exp - 1)
    order = jnp.argsort(e, stable=True).astype(jnp.int32)
    first = jnp.cumsum(counts) - counts
    r = jnp.arange(n_tiles * MOE_TILE, dtype=jnp.int32)
    er = tile_expert[r // MOE_TILE]
    off = r - tile_start[er] * MOE_TILE
    valid = jnp.logical_and(off < counts[er], r // MOE_TILE < n_used)
    src = jnp.where(valid, order[jnp.clip(first[er] + off, 0, t * k - 1)] // k, 0)
    return dest.astype(jnp.int32), src.astype(jnp.int32), tile_expert, tile_block.astype(jnp.int32), \
        n_used.reshape(1).astype(jnp.int32)


def _start_row_copies(idx_ref, base, n_rows, src_hbm, buf, slot, sem):
    def body(pair, carry):
        for queue in range(N_DMA_QUEUES):
            r = N_DMA_QUEUES * pair + queue
            row = idx_ref[base + r]
            pltpu.make_async_copy(src_hbm.at[pl.ds(row, 1)], buf.at[slot, pl.ds(r, 1)],
                                  sem.at[slot]).start(priority=queue)
        return carry
    lax.fori_loop(0, n_rows // N_DMA_QUEUES, body, 0, unroll=4)


def _wait_row_copies(src_hbm, buf, slot, sem):
    pltpu.make_async_copy(src_hbm.at[pl.ds(0, buf.shape[1])], buf.at[slot], sem.at[slot]).wait()


def _row_gather_pipeline(idx_ref, n_rows, src_hbm, buf, sem, n_steps):
    s = pl.program_id(0)
    slot = s % 2

    @pl.when(s == 0)
    def _():
        _start_row_copies(idx_ref, 0, n_rows, src_hbm, buf, 0, sem)

    @pl.when(s + 1 < n_steps)
    def _():
        _start_row_copies(idx_ref, (s + 1) * n_rows, n_rows, src_hbm, buf, 1 - slot, sem)

    @pl.when(s < n_steps)
    def _():
        _wait_row_copies(src_hbm, buf, slot, sem)

    return slot


def _gather_kernel(idx_ref, n_steps_ref, src_hbm, o_ref, buf, sem):
    n_steps = n_steps_ref[0]
    slot = _row_gather_pipeline(idx_ref, o_ref.shape[0], src_hbm, buf, sem, n_steps)
    used = pl.program_id(0) < n_steps

    @pl.when(used)
    def _():
        o_ref[...] = buf[slot].astype(o_ref.dtype)

    @pl.when(jnp.logical_not(used))
    def _():
        o_ref[...] = jnp.zeros(o_ref.shape, o_ref.dtype)


def _gather_rows(src, idx, n_rows_used, out_dtype):
    n, d = idx.shape[0], src.shape[1]
    g = GATHER_ROWS
    return pl.pallas_call(
        _gather_kernel,
        grid_spec=pltpu.PrefetchScalarGridSpec(
            num_scalar_prefetch=2, grid=(n // g,),
            in_specs=[pl.BlockSpec(memory_space=pl.ANY)],
            out_specs=pl.BlockSpec((g, d), lambda s, idx, ns: (s, 0)),
            scratch_shapes=[pltpu.VMEM((2, g, d), src.dtype), pltpu.SemaphoreType.DMA((2,))]),
        out_shape=jax.ShapeDtypeStruct((n, d), out_dtype),
        compiler_params=_params("arbitrary"),
        name="moe_gather",
    )(idx, n_rows_used // g, src)


def _tile_flags(te_ref, nu_ref):
    i = pl.program_id(1)
    changed = jnp.logical_or(i == 0, te_ref[i] != te_ref[jnp.maximum(i - 1, 0)])
    return changed, i < nu_ref[0]


def _moe_up_kernel(te_ref, tb_ref, nu_ref, x_ref, wg_ref, wu_ref, o_ref, wg_bf, wu_bf, *, d_exp):
    n = pl.program_id(0)
    tn = o_ref.shape[-1]
    changed, active = _tile_flags(te_ref, nu_ref)

    @pl.when(changed)
    def _():
        row = n * tn + lax.broadcasted_iota(jnp.int32, (tn, 1), 0)
        wg_bf[...] = jnp.where(row < d_exp, wg_ref[...], 0.0).astype(BF16)
        wu_bf[...] = jnp.where(row < d_exp, wu_ref[...], 0.0).astype(BF16)

    @pl.when(active)
    def _():
        x = x_ref[...]
        g = lax.dot_general(x, wg_bf[...], NT_DIMS, preferred_element_type=F32)
        u = lax.dot_general(x, wu_bf[...], NT_DIMS, preferred_element_type=F32)
        o_ref[...] = (_silu(g) * u).astype(BF16)

    @pl.when(jnp.logical_not(active))
    def _():
        o_ref[...] = jnp.zeros(o_ref.shape, BF16)


def _moe_up(xs, plan, w_gate_t, w_up_t, layer):
    _, _, tile_expert, tile_block, n_used = plan
    d = xs.shape[1]
    d_exp = w_gate_t.shape[2]
    n_tiles = tile_expert.shape[0]
    tn = MOE_UP_TN
    nt = pl.cdiv(d_exp, tn)
    wspec = pl.BlockSpec((None, None, tn, d), lambda n, i, te, tb, nu: (layer, te[i], n, 0))
    return pl.pallas_call(
        functools.partial(_moe_up_kernel, d_exp=d_exp),
        grid_spec=pltpu.PrefetchScalarGridSpec(
            num_scalar_prefetch=3, grid=(nt, n_tiles),
            in_specs=[pl.BlockSpec((MOE_TILE, d), lambda n, i, te, tb, nu: (tb[i], 0)), wspec, wspec],
            out_specs=pl.BlockSpec((MOE_TILE, tn), lambda n, i, te, tb, nu: (i, n)),
            scratch_shapes=[pltpu.VMEM((tn, d), BF16), pltpu.VMEM((tn, d), BF16)]),
        out_shape=jax.ShapeDtypeStruct((xs.shape[0], nt * tn), BF16),
        compiler_params=_params("arbitrary", "arbitrary"),
        name="moe_up",
    )(tile_expert, tile_block, n_used, xs, w_gate_t, w_up_t)


MOE_DOWN_TN = 1024
MOE_K_ALIGN = MXU_DIM


def _moe_down_kernel(te_ref, tb_ref, nu_ref, x_ref, w_ref, o_ref, w_bf):
    changed, active = _tile_flags(te_ref, nu_ref)
    k_valid = w_ref.shape[0]

    @pl.when(changed)
    def _():
        w_bf[:k_valid, :] = w_ref[...].astype(BF16)
        w_bf[k_valid:, :] = jnp.zeros((w_bf.shape[0] - k_valid, w_bf.shape[1]), BF16)

    @pl.when(active)
    def _():
        o_ref[...] = _bdot(x_ref[...], w_bf[...])

    @pl.when(jnp.logical_not(active))
    def _():
        o_ref[...] = jnp.zeros(o_ref.shape, F32)


def _moe_down(act, plan, w_down, layer):
    _, _, tile_expert, tile_block, n_used = plan
    k_valid, n = w_down.shape[2:]
    k_pad = pl.cdiv(k_valid, MOE_K_ALIGN) * MOE_K_ALIGN
    assert k_pad <= act.shape[1]
    n_tiles = tile_expert.shape[0]
    tn = MOE_DOWN_TN
    return pl.pallas_call(
        _moe_down_kernel,
        grid_spec=pltpu.PrefetchScalarGridSpec(
            num_scalar_prefetch=3, grid=(n // tn, n_tiles),
            in_specs=[pl.BlockSpec((MOE_TILE, k_pad), lambda j, i, te, tb, nu: (tb[i], 0)),
                      pl.BlockSpec((None, None, k_valid, tn), lambda j, i, te, tb, nu: (layer, te[i], 0, j))],
            out_specs=pl.BlockSpec((MOE_TILE, tn), lambda j, i, te, tb, nu: (i, j)),
            scratch_shapes=[pltpu.VMEM((k_pad, tn), BF16)]),
        out_shape=jax.ShapeDtypeStruct((act.shape[0], n), F32),
        compiler_params=_params("arbitrary", "arbitrary"),
        name="moe_down",
    )(tile_expert, tile_block, n_used, act, w_down)


def _moe_combine_kernel(pos_ref, x_ref, wt_ref, gpost_ref, gate_ref, y_hbm, *rest, tok):
    xo_refs, (buf, sem) = rest[:-2], rest[-2:]
    tm = x_ref.shape[0]
    slot = _row_gather_pipeline(pos_ref, TOP_K * tm, y_hbm, buf, sem, pl.num_programs(0))
    wt = wt_ref[...]
    f = wt[:, 0:1] * buf[slot, pl.ds(0, tm), :]
    for k in range(1, TOP_K):
        f = f + wt[:, k:k + 1] * buf[slot, pl.ds(k * tm, tm), :]
    _store_rows(tok, xo_refs, x_ref[...] + gate_ref[...] * _rms(f, gpost_ref[...]))


def _moe_combine(tok, x, y, dest, wt, g_post, mod5, layer, seg_gate, split_out=False):
    t, d = x.shape
    tm = NORM_TILE
    pos = dest.reshape(t // tm, tm, TOP_K).transpose(0, 2, 1).reshape(-1)
    row = pl.BlockSpec((tm, d), lambda i, pos: (i, 0))
    return pl.pallas_call(
        functools.partial(_moe_combine_kernel, tok=tok),
        grid_spec=pltpu.PrefetchScalarGridSpec(
            num_scalar_prefetch=1, grid=(t // tm,),
            in_specs=[row, pl.BlockSpec((tm, TOP_K), lambda i, pos: (i, 0)),
                      pl.BlockSpec((None, 1, d), lambda i, pos: (layer, 0, 0)),
                      pl.BlockSpec((None, None, None, 1, d),
                                   lambda i, pos: (layer, tok.cond_row(i, tm), seg_gate, 0, 0)),
                      pl.BlockSpec(memory_space=pl.ANY)],
            out_specs=_split_row_specs(tok, d) if split_out else [row],
            scratch_shapes=[pltpu.VMEM((2, TOP_K * tm, d), F32), pltpu.SemaphoreType.DMA((2,))]),
        out_shape=_split_shapes(tok, d) if split_out else [jax.ShapeDtypeStruct((t, d), F32)],
        compiler_params=_params("arbitrary"),
        name="moe_combine",
    )(pos, x, wt, g_post.reshape(g_post.shape[0], 1, d), mod5, y)


SEG_SHIFT_MIX, SEG_SCALE_MIX, SEG_GATE_MIX, SEG_SHIFT_FFN, SEG_SCALE_FFN, SEG_GATE_FFN = range(N_MOD)


def kernel(x_prompt, x_sample, c, c_ctx, cache_k, cache_v, state_ssm_re, state_ssm_im, w_mod, b_mod, g_mix_pre, g_mix_post, g_ffn_pre, g_ffn_post, w_in, g_q, g_k, ssm_a_re, ssm_a_im, ssm_log_dt, ssm_b_re, ssm_b_im, ssm_c_re, ssm_c_im, ssm_d, w_glu, b_glu, conv_w, w_out, w_ffn_gate, w_ffn_up, w_ffn_down, w_router, b_router, w_exp_gate, w_exp_up, w_exp_down):
    n_ctx_seq, ctx_len, d = x_prompt.shape
    n_lat_seq, lat_len, _ = x_sample.shape
    depth = w_in.shape[0]
    tok = _Tokens(n_ctx_seq, ctx_len, n_lat_seq, lat_len)
    attn_w = d // 2
    kv_w = attn_w // Q_PER_KV
    ssm_w = ssm_d.shape[-1]
    n_kv = kv_w // HEAD_DIM
    n_dir, n_grp, n_state = ssm_a_re.shape[1:]
    nb = n_grp // SSM_GROUPS_PER_BLOCK
    sw = 2 * SSM_GROUPS_PER_BLOCK * n_state

    x = (x_prompt.reshape(tok.n_ctx, d), x_sample.reshape(tok.n_lat, d))
    cond = jnp.concatenate([c_ctx[None, :], c, jnp.zeros((8 - 1 - n_lat_seq, d), F32)], axis=0)
    mod = _modulation(cond, w_mod, b_mod)
    mod5 = mod.reshape(depth, cond.shape[0], N_MOD, 1, d)
    cos, sin = _rope_tables(tok)
    ck = cache_k.reshape(cache_k.shape[:3] + (kv_w,))
    cv = cache_v.reshape(cache_v.shape[:3] + (kv_w,))

    h = _prenorm(tok, x, g_mix_pre, mod5, 0, SEG_SHIFT_MIX, SEG_SCALE_MIX)
    new_k, new_v, new_re, new_im = [], [], [], []
    for l in range(depth):
        proj = _matmul(tok, [(h, 0)], w_in, l, F32)
        qn, kf, kb, vb = _qkv_prep(tok, proj, g_q, g_k, l, attn_w, kv_w, cos, sin)
        attn = (_attention_ctx(tok, qn, kb, vb), _attention_lat(tok, qn, kb, vb, ck, cv, l))
        new_k.append(kf[:tok.n_ctx].reshape(n_ctx_seq, ctx_len, n_kv, HEAD_DIM))
        new_v.append(proj[:tok.n_ctx, attn_w + kv_w:attn_w + 2 * kv_w].reshape(n_ctx_seq, ctx_len, n_kv, HEAD_DIM))

        u_col = attn_w + 2 * kv_w
        kt, e2, fc, a_q = _ssm_tables(ssm_a_re[l], ssm_a_im[l], ssm_log_dt[l], ssm_b_re[l], ssm_b_im[l],
                                      ssm_c_re[l], ssm_c_im[l])
        h0 = jnp.concatenate([state_ssm_re[:, l].reshape(n_lat_seq, n_dir, nb, sw // 2),
                              state_ssm_im[:, l].reshape(n_lat_seq, n_dir, nb, sw // 2)], axis=-1)
        ydir, fin = _ssm_scan(tok, proj, u_col // LANES, kt, e2, fc, a_q, h0.transpose(1, 2, 0, 3))
        fin = fin.transpose(2, 0, 1, 3).reshape(n_ctx_seq, n_dir, nb, 2, SSM_GROUPS_PER_BLOCK, n_state)
        new_re.append(fin[:, :, :, 0].reshape(n_ctx_seq, n_dir, n_grp, n_state))
        new_im.append(fin[:, :, :, 1].reshape(n_ctx_seq, n_dir, n_grp, n_state))
        ssm_out = _ssm_glu(proj, u_col // ssm_w, ydir, ssm_d, w_glu, b_glu, l)

        conv_out = _conv(tok, proj, (u_col + ssm_w) // ssm_w, conv_w, l)
        mix = _matmul(tok, [(attn, 0), (ssm_out, attn_w // ssm_w), (conv_out, attn_w // ssm_w + 1)],
                      w_out, l, F32)
        dense = l % 2 == 0
        x, h, *routing = _resid_norm(
            tok, x, mix, g_mix_post, mod5, l, SEG_GATE_MIX, nxt=(g_ffn_pre, l, SEG_SHIFT_FFN, SEG_SCALE_FFN),
            h_dtype=BF16 if dense else F32,
            router=None if dense else (jnp.swapaxes(w_router, 1, 2), b_router, l // 2))
        last = l + 1 == depth
        nxt = None if last else (g_mix_pre, l + 1, SEG_SHIFT_MIX, SEG_SCALE_MIX)
        if dense:
            act = _swiglu_up(h, w_ffn_gate, w_ffn_up, l // 2)
            f = _down(act, w_ffn_down.reshape((w_ffn_down.shape[0], 1) + w_ffn_down.shape[1:]), l // 2)
            outs = _resid_norm(tok, x, f, g_ffn_post, mod5, l, SEG_GATE_FFN, nxt=nxt, split_out=last)
            x, h = (tuple(outs), None) if last else outs
        else:
            m = l // 2
            idx, wt = routing
            plan = _moe_plan(idx, w_router.shape[-1])
            xs = _gather_rows(h, plan[1], plan[4] * MOE_TILE, BF16)
            act = _moe_up(xs, plan, jnp.swapaxes(w_exp_gate, 2, 3), jnp.swapaxes(w_exp_up, 2, 3), m)
            y = _moe_down(act, plan, w_exp_down, m)
            outs = _moe_combine(tok, x, y, plan[0], wt, g_ffn_post, mod5, l, SEG_GATE_FFN, split_out=last)
            if last:
                x = tuple(outs)
            else:
                (x,) = outs
                h = _prenorm(tok, x, nxt[0], mod5, nxt[1], nxt[2], nxt[3])

    return (x[0].reshape(x_prompt.shape), x[1].reshape(x_sample.shape),
            jnp.stack(new_k, axis=1), jnp.stack(new_v, axis=1),
            jnp.stack(new_re, axis=1), jnp.stack(new_im, axis=1))
```

```python
import functools
import math

import jax
import jax.numpy as jnp
import numpy as np
from jax import lax
from jax.experimental import pallas as pl
from jax.experimental.pallas import tpu as pltpu

F32 = jnp.float32
BF16 = jnp.bfloat16

HEAD_DIM = 128
Q_PER_KV = 4
GRID_W = 64
SSM_GROUP = 16
N_MOD = 6
TOP_K = 2
ROPE_THETA = 10000.0
EPS = 1e-6

LANES = 128
MXU_DIM = 256
VMEM_LIMIT_BYTES = 56 * 1024 * 1024

SSM_CHUNK = 8
SSM_SCAN_UNROLL = 4
SSM_GROUPS_PER_BLOCK = LANES // SSM_GROUP


def _params(*semantics):
    return pltpu.CompilerParams(dimension_semantics=semantics, vmem_limit_bytes=VMEM_LIMIT_BYTES)


def _rms(x, g):
    return x * lax.rsqrt(jnp.mean(x * x, axis=-1, keepdims=True) + EPS) * g


def _silu(x):
    return x * jax.nn.sigmoid(x)


def _bdot(a, b):
    return jnp.dot(a, b, preferred_element_type=F32)


def _pack_bf16_pairs(x):
    half = x.shape[1] // 2
    lo = lax.bitcast_convert_type(x[:, :half].astype(BF16).astype(F32), jnp.uint32)
    hi = lax.bitcast_convert_type(x[:, half:].astype(BF16).astype(F32), jnp.uint32)
    return (lo >> 16) | (hi & jnp.uint32(0xFFFF0000))


def _unpack_bf16_pairs(p):
    lo = lax.bitcast_convert_type(p << 16, F32)
    hi = lax.bitcast_convert_type(p & jnp.uint32(0xFFFF0000), F32)
    return jnp.concatenate([lo, hi], axis=1).astype(BF16)


class _Tokens:
    def __init__(self, n_ctx_seq, ctx_len, n_lat_seq, lat_len):
        self.n_ctx_seq, self.ctx_len = n_ctx_seq, ctx_len
        self.n_lat_seq, self.lat_len = n_lat_seq, lat_len
        self.n_ctx = n_ctx_seq * ctx_len
        self.n_lat = n_lat_seq * lat_len
        self.total = self.n_ctx + self.n_lat

    def cond_row(self, i, tile):
        ctx_tiles = self.n_ctx // tile
        lat_tiles = self.lat_len // tile
        return jnp.where(i < ctx_tiles, 0, 1 + (i - ctx_tiles) // lat_tiles)


def _mod_kernel(c_ref, w_ref, b_ref, o_ref):
    s = _silu(c_ref[...]).astype(BF16)
    o_ref[...] = _bdot(s, w_ref[...].astype(BF16)) + b_ref[...]


def _modulation(cond, w_mod, b_mod):
    depth, d, n = w_mod.shape
    rows = cond.shape[0]
    tn = 512
    return pl.pallas_call(
        _mod_kernel,
        grid=(depth, n // tn),
        in_specs=[
            pl.BlockSpec((rows, d), lambda l, j: (0, 0)),
            pl.BlockSpec((None, d, tn), lambda l, j: (l, 0, j)),
            pl.BlockSpec((None, 1, tn), lambda l, j: (l, 0, j)),
        ],
        out_specs=pl.BlockSpec((None, rows, tn), lambda l, j: (l, 0, j)),
        out_shape=jax.ShapeDtypeStruct((depth, rows, n), F32),
        compiler_params=_params("parallel", "parallel"),
        name="modulation",
    )(cond, w_mod, b_mod.reshape(depth, 1, n))


NORM_TILE = 256


def _mod_spec(tok, layer, seg, d):
    return pl.BlockSpec((None, None, None, 1, d),
                        lambda i: (layer, tok.cond_row(i, NORM_TILE), seg, 0, 0))


def _split_row_specs(tok, d):
    ct = tok.n_ctx // NORM_TILE
    return [pl.BlockSpec((NORM_TILE, d), lambda i, *_: (jnp.minimum(i, ct - 1), 0)),
            pl.BlockSpec((NORM_TILE, d), lambda i, *_: (jnp.maximum(i - ct, 0), 0))]


def _split_shapes(tok, d):
    return [jax.ShapeDtypeStruct((tok.n_ctx, d), F32), jax.ShapeDtypeStruct((tok.n_lat, d), F32)]


def _load_rows(tok, refs):
    if len(refs) == 1:
        return refs[0][...]
    return jnp.where(pl.program_id(0) < tok.n_ctx // NORM_TILE, refs[0][...], refs[1][...])


def _store_rows(tok, refs, val):
    if len(refs) == 1:
        refs[0][...] = val
        return
    is_ctx = pl.program_id(0) < tok.n_ctx // NORM_TILE

    @pl.when(is_ctx)
    def _():
        refs[0][...] = val

    @pl.when(jnp.logical_not(is_ctx))
    def _():
        refs[1][...] = val


def _as_rows(x):
    return list(x) if isinstance(x, (tuple, list)) else [x]


def _prenorm_kernel(*refs, tok, n_x):
    g_ref, sc_ref, sh_ref, h_ref = refs[n_x:]
    h = _rms(_load_rows(tok, refs[:n_x]), g_ref[...]) * (1.0 + sc_ref[...]) + sh_ref[...]
    h_ref[...] = h.astype(BF16)


def _prenorm(tok, x, g, mod5, layer, seg_shift, seg_scale):
    xs = _as_rows(x)
    d = xs[0].shape[1]
    row = pl.BlockSpec((NORM_TILE, d), lambda i: (i, 0))
    return pl.pallas_call(
        functools.partial(_prenorm_kernel, tok=tok, n_x=len(xs)),
        grid=(tok.total // NORM_TILE,),
        in_specs=([row] if len(xs) == 1 else _split_row_specs(tok, d))
        + [pl.BlockSpec((None, 1, d), lambda i: (layer, 0, 0)),
           _mod_spec(tok, layer, seg_scale, d), _mod_spec(tok, layer, seg_shift, d)],
        out_specs=row,
        out_shape=jax.ShapeDtypeStruct((tok.total, d), BF16),
        compiler_params=_params("arbitrary"),
        name="prenorm",
    )(*xs, g.reshape(g.shape[0], 1, d), mod5, mod5)


def _resid_norm_kernel(*refs, tok, n_x, n_xo, with_next, with_router):
    y_ref, gpost_ref, gate_ref = refs[n_x:n_x + 3]
    rest = list(refs[n_x + 3:])
    x = _load_rows(tok, refs[:n_x]) + gate_ref[...] * _rms(y_ref[...], gpost_ref[...])
    if with_next:
        gpre_ref, sc_ref, sh_ref = rest[:3]
        rest = rest[3:]
        h = _rms(x, gpre_ref[...]) * (1.0 + sc_ref[...]) + sh_ref[...]
        if with_router:
            wr_ref, br_ref = rest[:2]
            rest = rest[2:]
            idx_ref, wt_ref = rest[n_xo + 1:]
            idx_ref[...], wt_ref[...] = _route(h, wr_ref[...], br_ref[...])
        h_ref = rest[n_xo]
        h_ref[...] = _pack_bf16_pairs(h) if h_ref.dtype == jnp.uint32 else h.astype(h_ref.dtype)
    _store_rows(tok, rest[:n_xo], x)


def _resid_norm(tok, x, y, g_post, mod5, layer, seg_gate, nxt=None, h_dtype=BF16, split_out=False, router=None):
    xs = _as_rows(x)
    t, d = y.shape
    row = pl.BlockSpec((NORM_TILE, d), lambda i: (i, 0))
    in_specs = ([row] if len(xs) == 1 else _split_row_specs(tok, d)) + [
        row, pl.BlockSpec((None, 1, d), lambda i: (layer, 0, 0)), _mod_spec(tok, layer, seg_gate, d)]
    args = xs + [y, g_post.reshape(g_post.shape[0], 1, d), mod5]
    out_specs = _split_row_specs(tok, d) if split_out else [row]
    out_shape = _split_shapes(tok, d) if split_out else [jax.ShapeDtypeStruct((t, d), F32)]
    n_xo = len(out_specs)
    if nxt is not None:
        g_pre, nl, seg_shift, seg_scale = nxt
        in_specs += [pl.BlockSpec((None, 1, d), lambda i: (nl, 0, 0)),
                     _mod_spec(tok, nl, seg_scale, d), _mod_spec(tok, nl, seg_shift, d)]
        args += [g_pre.reshape(g_pre.shape[0], 1, d), mod5, mod5]
        hw = d // 2 if h_dtype == jnp.uint32 else d
        out_specs.append(pl.BlockSpec((NORM_TILE, hw), lambda i: (i, 0)))
        out_shape.append(jax.ShapeDtypeStruct((t, hw), h_dtype))
    if router is not None:
        w_router_t, b_router, rl = router
        n_exp = w_router_t.shape[1]
        in_specs += [pl.BlockSpec((None, n_exp, d), lambda i: (rl, 0, 0)),
                     pl.BlockSpec((None, 1, n_exp), lambda i: (rl, 0, 0))]
        args += [w_router_t, b_router.reshape(-1, 1, n_exp)]
        out_specs += [pl.BlockSpec((NORM_TILE, TOP_K), lambda i: (i, 0))] * 2
        out_shape += [jax.ShapeDtypeStruct((t, TOP_K), jnp.int32), jax.ShapeDtypeStruct((t, TOP_K), F32)]
    return pl.pallas_call(
        functools.partial(_resid_norm_kernel, tok=tok, n_x=len(xs), n_xo=n_xo, with_next=nxt is not None,
                          with_router=router is not None),
        grid=(t // NORM_TILE,),
        in_specs=in_specs, out_specs=out_specs, out_shape=out_shape,
        compiler_params=_params("arbitrary"),
        name="resid_norm",
    )(*args)


MM_TM = 1024
MM_TN = 512


def _mm_kernel(*refs, n_parts, ctx_tiles):
    o_ref = refs[-1]
    n_x = sum(n_parts)
    acc, at = None, 0
    for s, parts in enumerate(n_parts):
        x = refs[at][...]
        if parts == 2:
            x = jnp.where(pl.program_id(0) < ctx_tiles, x, refs[at + 1][...])
        at += parts
        p = _bdot(x, refs[n_x + s][...].astype(BF16))
        acc = p if acc is None else acc + p
    o_ref[...] = acc.astype(o_ref.dtype)


def _matmul(tok, segments, w, layer, out_dtype, tn=MM_TN):
    m = tok.total
    n = w.shape[-1]
    ct = tok.n_ctx // MM_TM
    in_specs, args, n_parts = [], [], []
    for x, _ in segments:
        xs = _as_rows(x)
        k = xs[0].shape[1]
        if len(xs) == 1:
            in_specs.append(pl.BlockSpec((MM_TM, k), lambda i, j: (i, 0)))
        else:
            in_specs += [pl.BlockSpec((MM_TM, k), lambda i, j: (jnp.minimum(i, ct - 1), 0)),
                         pl.BlockSpec((MM_TM, k), lambda i, j: (jnp.maximum(i - ct, 0), 0))]
        args += xs
        n_parts.append(len(xs))
    for x, rb in segments:
        k = _as_rows(x)[0].shape[1]
        in_specs.append(pl.BlockSpec((None, k, tn), lambda i, j, rb=rb: (layer, rb, j)))
        args.append(w)
    return pl.pallas_call(
        functools.partial(_mm_kernel, n_parts=tuple(n_parts), ctx_tiles=ct),
        grid=(m // MM_TM, n // tn),
        in_specs=in_specs,
        out_specs=pl.BlockSpec((MM_TM, tn), lambda i, j: (i, j)),
        out_shape=jax.ShapeDtypeStruct((m, n), out_dtype),
        compiler_params=_params("parallel", "parallel"),
        name="matmul",
    )(*args)


def _swiglu_up_kernel(x_ref, wg_ref, wu_ref, o_ref):
    x = x_ref[...]
    g = _bdot(x, wg_ref[...].astype(BF16))
    u = _bdot(x, wu_ref[...].astype(BF16))
    o_ref[...] = (_silu(g) * u).astype(BF16)


def _swiglu_up(h, w_gate, w_up, layer):
    m, d = h.shape
    n = w_gate.shape[-1]
    tn = 256
    wspec = pl.BlockSpec((None, d, tn), lambda i, j: (layer, 0, j))
    return pl.pallas_call(
        _swiglu_up_kernel,
        grid=(m // MM_TM, n // tn),
        in_specs=[pl.BlockSpec((MM_TM, d), lambda i, j: (i, 0)), wspec, wspec],
        out_specs=pl.BlockSpec((MM_TM, tn), lambda i, j: (i, j)),
        out_shape=jax.ShapeDtypeStruct((m, n), BF16),
        compiler_params=_params("parallel", "parallel"),
        name="swiglu_up",
    )(h, w_gate, w_up)


DOWN_TK = 256


def _down_kernel(x_ref, w_ref, o_ref, *, k_valid):
    k = pl.program_id(2)

    @pl.when(jnp.logical_and(pl.program_id(1) == 0, k == 0))
    def _():
        o_ref[...] = jnp.zeros(o_ref.shape, F32)

    w = w_ref[...]
    if k_valid % DOWN_TK:
        row = k * DOWN_TK + lax.broadcasted_iota(jnp.int32, (DOWN_TK, 1), 0)
        w = jnp.where(row < k_valid, w, 0.0)
    w = w.astype(BF16)
    x = x_ref[...]
    for c in range(0, o_ref.shape[1], MM_TN):
        o_ref[:, c:c + MM_TN] += _bdot(x, w[:, c:c + MM_TN])


def _down(act, w, layer):
    m = act.shape[0]
    n_grp, k_valid, n = w.shape[-3:]
    kt = pl.cdiv(k_valid, DOWN_TK)
    assert act.shape[1] == n_grp * kt * DOWN_TK
    return pl.pallas_call(
        functools.partial(_down_kernel, k_valid=k_valid),
        grid=(m // MM_TM, n_grp, kt),
        in_specs=[pl.BlockSpec((MM_TM, DOWN_TK), lambda i, g, k: (i, g * kt + k)),
                  pl.BlockSpec((None, None, DOWN_TK, n), lambda i, g, k: (layer, g, k, 0))],
        out_specs=pl.BlockSpec((MM_TM, n), lambda i, g, k: (i, 0)),
        out_shape=jax.ShapeDtypeStruct((m, n), F32),
        compiler_params=_params("parallel", "arbitrary", "arbitrary"),
        name="down",
    )(act, w)


ATTN_TILE = 256
SOFTMAX_Q_SCALE = HEAD_DIM ** -0.5 * math.log2(math.e)
CTX_HEADS_PER_CHAIN = 2
LAT_HEADS_PER_CHAIN = 1


def _rope_tables(tok):
    pos = np.arange(tok.lat_len)
    half = HEAD_DIM // 2
    inv = ROPE_THETA ** (-np.arange(0, half, 2, dtype=np.float64) / half)
    ang_r = (pos // GRID_W)[:, None] * inv[None, :]
    ang_c = (pos % GRID_W)[:, None] * inv[None, :]
    cos = np.concatenate([np.cos(ang_r)] * 2 + [np.cos(ang_c)] * 2, axis=-1)
    sin = np.concatenate([-np.sin(ang_r), np.sin(ang_r), -np.sin(ang_c), np.sin(ang_c)], axis=-1)
    ident = np.ones((ATTN_TILE, HEAD_DIM), np.float32)
    return (jnp.asarray(np.concatenate([ident, cos], axis=0), F32),
            jnp.asarray(np.concatenate([0.0 * ident, sin], axis=0), F32))


def _qkv_prep_kernel(q_ref, k_ref, v_ref, gq_ref, gk_ref, cos_ref, sin_ref,
                     qo_ref, kf_ref, kb_ref, vb_ref, *, ctx_tiles):
    def run(rope):
        if rope:
            cos, sin = cos_ref[...], sin_ref[...]
            lane = lax.broadcasted_iota(jnp.int32, cos.shape, 1)
            first = (lane % (HEAD_DIM // 2)) < (HEAD_DIM // 4)
            quarter = HEAD_DIM // 4

        def norm_rope(x, g):
            xn = _rms(x, g)
            if not rope:
                return xn
            partner = jnp.where(first, pltpu.roll(xn, HEAD_DIM - quarter, 1), pltpu.roll(xn, quarter, 1))
            return xn * cos + partner * sin

        gq, gk = gq_ref[...], gk_ref[...]
        for h in range(q_ref.shape[1] // HEAD_DIM):
            sl = slice(h * HEAD_DIM, (h + 1) * HEAD_DIM)
            qo_ref[:, sl] = (norm_rope(q_ref[:, sl], gq) * SOFTMAX_Q_SCALE).astype(BF16)
        for h in range(k_ref.shape[1] // HEAD_DIM):
            sl = slice(h * HEAD_DIM, (h + 1) * HEAD_DIM)
            kn = norm_rope(k_ref[:, sl], gk)
            kf_ref[:, sl] = kn
            kb_ref[:, sl] = kn.astype(BF16)

    is_ctx = pl.program_id(0) < ctx_tiles
    pl.when(is_ctx)(functools.partial(run, False))
    pl.when(jnp.logical_not(is_ctx))(functools.partial(run, True))
    vb_ref[...] = v_ref[...].astype(BF16)


def _qkv_prep(tok, proj, g_q, g_k, layer, attn_w, kv_w, cos, sin):
    t = proj.shape[0]
    tm = ATTN_TILE
    ctx_tiles = tok.n_ctx // tm
    lat_tiles = tok.lat_len // tm

    def table_row(i):
        return jnp.where(i < ctx_tiles, 0, 1 + (i - ctx_tiles) % lat_tiles)

    kv_blk = attn_w // kv_w
    gspec = pl.BlockSpec((None, 1, HEAD_DIM), lambda i: (layer, 0, 0))
    tspec = pl.BlockSpec((tm, HEAD_DIM), lambda i: (table_row(i), 0))
    kvspec = pl.BlockSpec((tm, kv_w), lambda i: (i, 0))
    return pl.pallas_call(
        functools.partial(_qkv_prep_kernel, ctx_tiles=ctx_tiles),
        grid=(t // tm,),
        in_specs=[pl.BlockSpec((tm, attn_w), lambda i: (i, 0)),
                  pl.BlockSpec((tm, kv_w), lambda i: (i, kv_blk)),
                  pl.BlockSpec((tm, kv_w), lambda i: (i, kv_blk + 1)),
                  gspec, gspec, tspec, tspec],
        out_specs=[pl.BlockSpec((tm, attn_w), lambda i: (i, 0)), kvspec, kvspec, kvspec],
        out_shape=[jax.ShapeDtypeStruct((t, attn_w), BF16), jax.ShapeDtypeStruct((t, kv_w), F32),
                   jax.ShapeDtypeStruct((t, kv_w), BF16), jax.ShapeDtypeStruct((t, kv_w), BF16)],
        compiler_params=_params("parallel"),
        name="qkv_prep",
    )(proj, proj, proj, g_q.reshape(-1, 1, HEAD_DIM), g_k.reshape(-1, 1, HEAD_DIM), cos, sin)


def _attn_kernel(q_ref, *refs, n_kv, heads_per_chain):
    o_ref = refs[-1]
    tq = q_ref.shape[0]
    ks = [refs[2 * p][...].astype(BF16) for p in range(n_kv)]
    vs = [refs[2 * p + 1][...].astype(BF16) for p in range(n_kv)]
    for g0 in range(0, Q_PER_KV, heads_per_chain):
        heads = range(g0, g0 + heads_per_chain)
        q = jnp.concatenate([q_ref[:, g * HEAD_DIM:(g + 1) * HEAD_DIM] for g in heads], axis=0)
        ss = [lax.dot_general(q, k, (((1,), (1,)), ((), ())), preferred_element_type=F32) for k in ks]
        m = functools.reduce(jnp.maximum, [jnp.max(s, axis=-1, keepdims=True) for s in ss])
        ps = [jnp.exp2(s - m) for s in ss]
        denom = functools.reduce(jnp.add, [jnp.sum(p, axis=-1, keepdims=True) for p in ps])
        o = functools.reduce(jnp.add, [_bdot(p.astype(BF16), v) for p, v in zip(ps, vs)])
        o = o / denom
        for n, g in enumerate(heads):
            o_ref[:, g * HEAD_DIM:(g + 1) * HEAD_DIM] = o[n * tq:(n + 1) * tq].astype(BF16)


def _attention_ctx(tok, qn, kb, vb):
    L = tok.ctx_len
    n_kv = kb.shape[1] // HEAD_DIM
    qw = Q_PER_KV * HEAD_DIM
    kvspec = pl.BlockSpec((L, HEAD_DIM), lambda s, h: (s, h))
    qspec = pl.BlockSpec((L, qw), lambda s, h: (s, h))
    return pl.pallas_call(
        functools.partial(_attn_kernel, n_kv=1, heads_per_chain=CTX_HEADS_PER_CHAIN),
        grid=(tok.n_ctx_seq, n_kv),
        in_specs=[qspec, kvspec, kvspec],
        out_specs=qspec,
        out_shape=jax.ShapeDtypeStruct((tok.n_ctx, qn.shape[1]), BF16),
        compiler_params=_params("parallel", "parallel"),
        name="attn_ctx",
    )(qn, kb, vb)


def _attention_lat(tok, qn, kb, vb, cache_k, cache_v, layer):
    L = tok.lat_len
    n_kv = kb.shape[1] // HEAD_DIM
    qw = Q_PER_KV * HEAD_DIM
    tq = ATTN_TILE
    q0 = tok.n_ctx // tq
    k0 = tok.n_ctx // L
    past = cache_k.shape[2]
    cspec = pl.BlockSpec((None, None, past, HEAD_DIM), lambda b, h, i: (b, layer, 0, h))
    kvspec = pl.BlockSpec((L, HEAD_DIM), lambda b, h, i: (k0 + b, h))
    return pl.pallas_call(
        functools.partial(_attn_kernel, n_kv=2, heads_per_chain=LAT_HEADS_PER_CHAIN),
        grid=(tok.n_lat_seq, n_kv, L // tq),
        in_specs=[pl.BlockSpec((tq, qw), lambda b, h, i: (q0 + b * (L // tq) + i, h)),
                  cspec, cspec, kvspec, kvspec],
        out_specs=pl.BlockSpec((tq, qw), lambda b, h, i: (b * (L // tq) + i, h)),
        out_shape=jax.ShapeDtypeStruct((tok.n_lat, qn.shape[1]), BF16),
        compiler_params=_params("parallel", "parallel", "parallel"),
        name="attn_lat",
    )(qn, cache_k, cache_v, kb, vb)


def _ssm_tables(a_re, a_im, log_dt, b_re, b_im, c_re, c_im):
    hi = lax.Precision.HIGHEST
    n_dir, G, P = a_re.shape
    Q, gb = SSM_CHUNK, SSM_GROUPS_PER_BLOCK
    nb = G // gb
    dt = jnp.exp(log_dt)[None, :, :, None]
    ks = jnp.arange(Q + 1, dtype=F32)[:, None, None, None]
    mag = jnp.exp(ks * dt * a_re[None])
    ang = ks * dt * a_im[None]
    pr, pi = mag * jnp.cos(ang), mag * jnp.sin(ang)
    nr, ni = pr[1] - 1.0, pi[1]
    den = a_re * a_re + a_im * a_im
    qr, qi = (nr * a_re + ni * a_im) / den, (ni * a_re - nr * a_im) / den
    bt_re, bt_im = jnp.swapaxes(b_re, -1, -2), jnp.swapaxes(b_im, -1, -2)
    bbr = qr[:, :, None, :] * bt_re - qi[:, :, None, :] * bt_im
    bbi = qr[:, :, None, :] * bt_im + qi[:, :, None, :] * bt_re
    pkr, pki = pr[:Q, :, :, None, :], pi[:Q, :, :, None, :]
    pbr = pkr * bbr[None] - pki * bbi[None]
    pbi = pkr * bbi[None] + pki * bbr[None]
    pcr, pci = pr[1:, :, :, None, :], pi[1:, :, :, None, :]
    car = c_re[None] * pcr - c_im[None] * pci
    cai = c_re[None] * pci + c_im[None] * pcr

    def rows128(re, im):
        t = jnp.concatenate([re, im], axis=-1)
        return t.reshape(t.shape[:-3] + (nb, LANES, 2 * P))

    def lead(t):
        return jnp.transpose(t, (1, 2, 0) + tuple(range(3, t.ndim)))

    e1 = rows128(pbr, pbi)
    kt = jnp.einsum('kdjan,djbn->kdjab', e1, rows128(c_re, -c_im), precision=hi)
    a_q = jnp.concatenate([pr[Q].reshape(n_dir, nb, 1, gb * P), pi[Q].reshape(n_dir, nb, 1, gb * P)], axis=-1)
    return lead(kt), lead(e1), lead(rows128(car, -cai)), a_q


def _ssm_kernel(u_ref, kt_ref, e1_ref, ca_ref, aq_ref, h0_ref, y_ref, fin_ref,
                we_scr, wy_scr, s_scr, hin_scr, *, tok):
    Q = SSM_CHUNK
    rows = u_ref.shape[0] // Q
    nlb = s_scr.shape[0]
    half = nlb // 2
    sw = nlb * LANES
    n_state = e1_ref.shape[2] // 2
    reverse = pl.program_id(0) == 1

    def lanes(b):
        return slice(b * LANES, (b + 1) * LANES)

    def iota(shape, dim):
        return lax.broadcasted_iota(jnp.int32, shape, dim)

    spread = jnp.where(jnp.logical_and(iota((LANES, sw), 0) // n_state == iota((LANES, sw), 1) // (sw // 2),
                                       iota((LANES, sw), 0) % n_state == iota((LANES, sw), 1) % n_state),
                       1.0, 0.0).astype(BF16)
    own_state = iota((LANES, sw), 0) // SSM_GROUP == (iota((LANES, sw), 1) % (sw // 2)) // n_state
    for s in range(Q):
        e1 = e1_ref[jnp.where(reverse, s, Q - 1 - s)].astype(BF16)
        we_scr[lanes(s), :] = jnp.where(own_state, _bdot(e1, spread), 0.0).astype(BF16)
    own_group = iota((LANES, LANES), 0) // SSM_GROUP == iota((LANES, LANES), 1) // SSM_GROUP
    for s in range(Q):
        for t in range(Q):
            if s == t:
                keep = own_group
            elif t > s:
                keep = jnp.logical_and(own_group, jnp.logical_not(reverse))
            else:
                keep = jnp.logical_and(own_group, reverse)
            wy_scr[lanes(s), lanes(t)] = jnp.where(keep, kt_ref[abs(t - s)], 0.0).astype(BF16)
    own_state_t = (iota((sw, LANES), 0) % (sw // 2)) // n_state == iota((sw, LANES), 1) // SSM_GROUP
    n_rep = sw // 2 // n_state
    for t in range(Q):
        fc = ca_ref[jnp.where(reverse, Q - 1 - t, t)].T
        tiled = jnp.concatenate([fc[:n_state]] * n_rep + [fc[n_state:]] * n_rep, axis=0)
        wy_scr[Q * LANES:, lanes(t)] = jnp.where(own_state_t, tiled, 0.0).astype(BF16)

    n_ctx_rows = tok.n_ctx // Q
    ctx_chunks, ns = tok.ctx_len // Q, tok.n_ctx_seq

    def ctx_tokens(c, t):
        return pl.ds(c * Q + t, ns, stride=tok.ctx_len)

    lat_tokens = [pl.ds(tok.n_ctx + t, rows - n_ctx_rows, stride=Q) for t in range(Q)]
    x = jnp.concatenate(
        [jnp.concatenate([u_ref[ctx_tokens(c, t), :] for c in range(ctx_chunks)] + [u_ref[lat_tokens[t], :]],
                         axis=0).astype(BF16) for t in range(Q)], axis=1)
    s = _bdot(x, we_scr[...])
    for b in range(nlb):
        s_scr[b] = s[:, lanes(b)]
    a = [aq_ref[:, lanes(b)] for b in range(nlb)]

    def scan(n_seq, n_chunk, row0, chunk_major, h_init):
        def step(i, h):
            c = jnp.where(reverse, n_chunk - 1 - i, i)
            if chunk_major:
                sel = pl.ds(pl.multiple_of(row0 + c * n_seq, n_seq), n_seq)
            else:
                sel = pl.ds(row0 + c, n_seq, stride=n_chunk)
            new = [None] * nlb
            for b in range(half):
                hr, hi = h[b], h[half + b]
                hin_scr[b, sel, :] = hr
                hin_scr[half + b, sel, :] = hi
                new[b] = a[b] * hr - a[half + b] * hi + s_scr[b, sel, :]
                new[half + b] = a[b] * hi + a[half + b] * hr + s_scr[half + b, sel, :]
            return tuple(new)
        return lax.fori_loop(0, n_chunk, step, h_init, unroll=SSM_SCAN_UNROLL)

    zeros = tuple(jnp.zeros((ns, LANES), F32) for _ in range(nlb))
    fin = scan(ns, ctx_chunks, 0, True, zeros)
    for b in range(nlb):
        fin_ref[:, lanes(b)] = fin[b]
    scan(tok.n_lat_seq, tok.lat_len // Q, n_ctx_rows, False, tuple(h0_ref[:, lanes(b)] for b in range(nlb)))
    xin = jnp.concatenate([x] + [hin_scr[b].astype(BF16) for b in range(nlb)], axis=1)
    y = _bdot(xin, wy_scr[...])
    for t in range(Q):
        for c in range(ctx_chunks):
            y_ref[ctx_tokens(c, t), :] = y[c * ns:(c + 1) * ns, lanes(t)]
        y_ref[lat_tokens[t], :] = y[n_ctx_rows:, lanes(t)]


def _ssm_scan(tok, proj, col_block0, tables, layer, h0):
    n_dir, nb = h0.shape[:2]
    t = proj.shape[0]
    Q = SSM_CHUNK
    rows = t // Q
    sw = h0.shape[-1]

    def table(x):
        return pl.BlockSpec((None, None, None) + x.shape[3:], lambda d, j: (layer, d, j) + (0,) * (x.ndim - 3))

    return pl.pallas_call(
        functools.partial(_ssm_kernel, tok=tok),
        grid=(n_dir, nb),
        in_specs=[pl.BlockSpec((t, LANES), lambda d, j: (0, col_block0 + j))] + [table(x) for x in tables]
        + [pl.BlockSpec((None, None) + h0.shape[2:], lambda d, j: (d, j, 0, 0))],
        out_specs=[pl.BlockSpec((None, t, LANES), lambda d, j: (d, 0, j)),
                   pl.BlockSpec((None, None, tok.n_ctx_seq, sw), lambda d, j: (d, j, 0, 0))],
        out_shape=[jax.ShapeDtypeStruct((n_dir, t, nb * LANES), F32),
                   jax.ShapeDtypeStruct((n_dir, nb, tok.n_ctx_seq, sw), F32)],
        scratch_shapes=[pltpu.VMEM((Q * LANES, sw), BF16), pltpu.VMEM((Q * LANES + sw, Q * LANES), BF16),
                        pltpu.VMEM((sw // LANES, rows, LANES), F32), pltpu.VMEM((sw // LANES, rows, LANES), F32)],
        compiler_params=_params("arbitrary", "arbitrary"),
        name="ssm_scan",
    )(proj, *tables, h0)


def _ssm_glu_kernel(u_ref, yf_ref, yb_ref, d_ref, w_ref, b_ref, o_ref):
    y = d_ref[...] * u_ref[...]
    y = y + yf_ref[...]
    y = y + yb_ref[...]
    z = jax.nn.gelu(y)
    gate = jax.nn.sigmoid(_bdot(z.astype(BF16), w_ref[...].astype(BF16)) + b_ref[...])
    o_ref[...] = (z * gate).astype(BF16)


def _ssm_glu(proj, col_block, ydir, ssm_d, w_glu, b_glu, layer):
    t = proj.shape[0]
    w = ydir.shape[-1]
    tm = 512
    row = pl.BlockSpec((None, 1, w), lambda i: (layer, 0, 0))
    return pl.pallas_call(
        _ssm_glu_kernel,
        grid=(t // tm,),
        in_specs=[pl.BlockSpec((tm, w), lambda i: (i, col_block)),
                  pl.BlockSpec((None, tm, w), lambda i: (0, i, 0)),
                  pl.BlockSpec((None, tm, w), lambda i: (1, i, 0)),
                  row, pl.BlockSpec((None, w, w), lambda i: (layer, 0, 0)), row],
        out_specs=pl.BlockSpec((tm, w), lambda i: (i, 0)),
        out_shape=jax.ShapeDtypeStruct((t, w), BF16),
        compiler_params=_params("parallel"),
        name="ssm_glu",
    )(proj, ydir, ydir, ssm_d.reshape(-1, 1, w), w_glu, b_glu.reshape(-1, 1, w))


CONV_TILE = 256
CONV_HALO = 8


def _conv_kernel(cx_ref, cb_ref, cc_ref, pcx_ref, pcc_ref, ncx_ref, ncc_ref, w_ref, o_ref, *, tok):
    i = pl.program_id(0)
    tm = CONV_TILE
    tiles_per_lat = tok.lat_len // tm
    ctx_tiles = tok.n_ctx // tm
    j = (i - ctx_tiles) % tiles_per_lat
    is_ctx = i < ctx_tiles
    has_prev = jnp.logical_and(jnp.logical_not(is_ctx), j > 0)
    has_next = jnp.logical_and(jnp.logical_not(is_ctx), j < tiles_per_lat - 1)
    z = cc_ref[...] * cx_ref[...]
    z_prev = jnp.where(has_prev, pcc_ref[CONV_HALO - 1:, :] * pcx_ref[CONV_HALO - 1:, :], 0.0)
    z_next = jnp.where(has_next, ncc_ref[:1, :] * ncx_ref[:1, :], 0.0)
    row = lax.broadcasted_iota(jnp.int32, z.shape, 0)
    up = jnp.where(row == 0, z_prev, pltpu.roll(z, 1, 0))
    dn = jnp.where(row == tm - 1, z_next, pltpu.roll(z, tm - 1, 0))
    y = up * w_ref[0:1, :] + z * w_ref[1:2, :] + dn * w_ref[2:3, :]
    o_ref[...] = (cb_ref[...] * y).astype(BF16)


def _conv(tok, proj, col_block0, conv_w, layer):
    t = proj.shape[0]
    w = conv_w.shape[-1]
    tm = CONV_TILE
    assert tok.ctx_len == tm and tok.lat_len % tm == 0
    hb = tm // CONV_HALO
    last = t // CONV_HALO - 1

    def main(c):
        return pl.BlockSpec((tm, w), lambda i: (i, col_block0 + c))

    def prev(c):
        return pl.BlockSpec((CONV_HALO, w), lambda i: (jnp.maximum(i * hb - 1, 0), col_block0 + c))

    def nxt(c):
        return pl.BlockSpec((CONV_HALO, w), lambda i: (jnp.minimum((i + 1) * hb, last), col_block0 + c))

    return pl.pallas_call(
        functools.partial(_conv_kernel, tok=tok),
        grid=(t // tm,),
        in_specs=[main(0), main(1), main(2), prev(0), prev(2), nxt(0), nxt(2),
                  pl.BlockSpec((None,) + conv_w.shape[1:], lambda i: (layer, 0, 0))],
        out_specs=pl.BlockSpec((tm, w), lambda i: (i, 0)),
        out_shape=jax.ShapeDtypeStruct((t, w), BF16),
        compiler_params=_params("parallel"),
        name="conv",
    )(proj, proj, proj, proj, proj, proj, proj, conv_w)


MOE_TILE = 512
MOE_UP_TN = 512
GATHER_ROWS = 256
N_DMA_QUEUES = 2
NT_DIMS = (((1,), (1,)), ((), ()))


def _route(h, w, b):
    hh, wh = h.astype(BF16), w.astype(BF16)
    hl, wl = (h - hh.astype(F32)).astype(BF16), (w - wh.astype(F32)).astype(BF16)

    def nt(x, y):
        return lax.dot_general(x, y, NT_DIMS, preferred_element_type=F32)

    lg = nt(hh, wh) + nt(hl, wh) + nt(hh, wl) + b
    n_exp = float(lg.shape[-1])
    idx = lax.broadcasted_iota(jnp.int32, lg.shape, 1).astype(F32)
    m1 = jnp.max(lg, axis=-1, keepdims=True)
    i1 = jnp.min(jnp.where(lg == m1, idx, n_exp), axis=-1, keepdims=True)
    rest = jnp.where(idx == i1, -jnp.inf, lg)
    m2 = jnp.max(rest, axis=-1, keepdims=True)
    i2 = jnp.min(jnp.where(rest == m2, idx, n_exp), axis=-1, keepdims=True)
    e = jnp.exp(m2 - m1)
    w1 = 1.0 / (1.0 + e)
    slot = lax.broadcasted_iota(jnp.int32, (h.shape[0], TOP_K), 1)
    return jnp.where(slot == 0, i1, i2).astype(jnp.int32), jnp.where(slot == 0, w1, e * w1)


def _moe_plan(idx, n_exp):
    t, k = idx.shape
    e = idx.reshape(-1)
    onehot = (e[:, None] == jnp.arange(n_exp, dtype=jnp.int32)[None, :]).astype(jnp.int32)
    csum = jnp.cumsum(onehot, axis=0)
    counts = csum[-1]
    rank = jnp.take_along_axis(csum, e[:, None], axis=1)[:, 0] - 1
    tiles_e = (counts + MOE_TILE - 1) // MOE_TILE
    tile_end = jnp.cumsum(tiles_e)
    tile_start = tile_end - tiles_e
    n_used = tile_end[-1]
    dest = tile_start[e] * MOE_TILE + rank
    n_tiles = (t * k) // MOE_TILE + n_exp
    ti = jnp.arange(n_tiles, dtype=jnp.int32)
    tile_block = jnp.minimum(ti, n_used - 1)
    tile_expert = jnp.sum((tile_end[None, :] <= tile_block[:, None]).astype(jnp.int32), axis=1)
    tile_expert = jnp.minimum(tile_expert, n_exp - 1)
    order = jnp.argsort(e, stable=True).astype(jnp.int32)
    first = jnp.cumsum(counts) - counts
    r = jnp.arange(n_tiles * MOE_TILE, dtype=jnp.int32)
    er = tile_expert[r // MOE_TILE]
    off = r - tile_start[er] * MOE_TILE
    valid = jnp.logical_and(off < counts[er], r // MOE_TILE < n_used)
    src = jnp.where(valid, order[jnp.clip(first[er] + off, 0, t * k - 1)] // k, 0)
    return dest.astype(jnp.int32), src.astype(jnp.int32), tile_expert, tile_block.astype(jnp.int32), \
        n_used.reshape(1).astype(jnp.int32)


def _start_row_copies(idx_ref, base, n_rows, src_hbm, buf, slot, sem):
    def body(pair, carry):
        for queue in range(N_DMA_QUEUES):
            r = N_DMA_QUEUES * pair + queue
            row = idx_ref[base + r]
            pltpu.make_async_copy(src_hbm.at[pl.ds(row, 1)], buf.at[slot, pl.ds(r, 1)],
                                  sem.at[slot]).start(priority=queue)
        return carry
    lax.fori_loop(0, n_rows // N_DMA_QUEUES, body, 0, unroll=4)


def _wait_row_copies(src_hbm, buf, slot, sem):
    pltpu.make_async_copy(src_hbm.at[pl.ds(0, buf.shape[1])], buf.at[slot], sem.at[slot]).wait()


def _row_gather_pipeline(idx_ref, n_rows, src_hbm, buf, sem, n_steps):
    s = pl.program_id(0)
    slot = s % 2

    @pl.when(s == 0)
    def _():
        _start_row_copies(idx_ref, 0, n_rows, src_hbm, buf, 0, sem)

    @pl.when(s + 1 < n_steps)
    def _():
        _start_row_copies(idx_ref, (s + 1) * n_rows, n_rows, src_hbm, buf, 1 - slot, sem)

    @pl.when(s < n_steps)
    def _():
        _wait_row_copies(src_hbm, buf, slot, sem)

    return slot


def _gather_kernel(idx_ref, n_steps_ref, src_hbm, o_ref, buf, sem):
    n_steps = n_steps_ref[0]
    slot = _row_gather_pipeline(idx_ref, o_ref.shape[0], src_hbm, buf, sem, n_steps)
    used = pl.program_id(0) < n_steps

    @pl.when(used)
    def _():
        o_ref[...] = _unpack_bf16_pairs(buf[slot])

    @pl.when(jnp.logical_not(used))
    def _():
        o_ref[...] = jnp.zeros(o_ref.shape, o_ref.dtype)


def _gather_rows(src, idx, n_rows_used):
    n, dp = idx.shape[0], src.shape[1]
    d = 2 * dp
    g = GATHER_ROWS
    return pl.pallas_call(
        _gather_kernel,
        grid_spec=pltpu.PrefetchScalarGridSpec(
            num_scalar_prefetch=2, grid=(n // g,),
            in_specs=[pl.BlockSpec(memory_space=pl.ANY)],
            out_specs=pl.BlockSpec((g, d), lambda s, idx, ns: (s, 0)),
            scratch_shapes=[pltpu.VMEM((2, g, dp), src.dtype), pltpu.SemaphoreType.DMA((2,))]),
        out_shape=jax.ShapeDtypeStruct((n, d), BF16),
        compiler_params=_params("arbitrary"),
        name="moe_gather",
    )(idx, n_rows_used // g, src)


def _tile_flags(te_ref, nu_ref):
    i = pl.program_id(1)
    changed = jnp.logical_or(i == 0, te_ref[i] != te_ref[jnp.maximum(i - 1, 0)])
    return changed, i < nu_ref[0]


def _moe_up_kernel(te_ref, tb_ref, nu_ref, x_ref, wg_ref, wu_ref, o_ref, wg_bf, wu_bf, *, d_exp):
    n = pl.program_id(0)
    tn = o_ref.shape[-1]
    changed, active = _tile_flags(te_ref, nu_ref)

    @pl.when(changed)
    def _():
        row = n * tn + lax.broadcasted_iota(jnp.int32, (tn, 1), 0)
        wg_bf[...] = jnp.where(row < d_exp, wg_ref[...], 0.0).astype(BF16)
        wu_bf[...] = jnp.where(row < d_exp, wu_ref[...], 0.0).astype(BF16)

    @pl.when(active)
    def _():
        x = x_ref[...]
        g = lax.dot_general(x, wg_bf[...], NT_DIMS, preferred_element_type=F32)
        u = lax.dot_general(x, wu_bf[...], NT_DIMS, preferred_element_type=F32)
        o_ref[...] = (_silu(g) * u).astype(BF16)

    @pl.when(jnp.logical_not(active))
    def _():
        o_ref[...] = jnp.zeros(o_ref.shape, BF16)


def _moe_up(xs, plan, w_gate_t, w_up_t, layer):
    _, _, tile_expert, tile_block, n_used = plan
    d = xs.shape[1]
    d_exp = w_gate_t.shape[2]
    n_tiles = tile_expert.shape[0]
    tn = MOE_UP_TN
    nt = pl.cdiv(d_exp, tn)
    wspec = pl.BlockSpec((None, None, tn, d), lambda n, i, te, tb, nu: (layer, te[i], n, 0))
    return pl.pallas_call(
        functools.partial(_moe_up_kernel, d_exp=d_exp),
        grid_spec=pltpu.PrefetchScalarGridSpec(
            num_scalar_prefetch=3, grid=(nt, n_tiles),
            in_specs=[pl.BlockSpec((MOE_TILE, d), lambda n, i, te, tb, nu: (tb[i], 0)), wspec, wspec],
            out_specs=pl.BlockSpec((MOE_TILE, tn), lambda n, i, te, tb, nu: (i, n)),
            scratch_shapes=[pltpu.VMEM((tn, d), BF16), pltpu.VMEM((tn, d), BF16)]),
        out_shape=jax.ShapeDtypeStruct((xs.shape[0], nt * tn), BF16),
        compiler_params=_params("arbitrary", "arbitrary"),
        name="moe_up",
    )(tile_expert, tile_block, n_used, xs, w_gate_t, w_up_t)


MOE_DOWN_TN = 1024
MOE_K_ALIGN = MXU_DIM


def _moe_down_kernel(te_ref, tb_ref, nu_ref, x_ref, w_ref, o_ref, w_bf):
    changed, active = _tile_flags(te_ref, nu_ref)
    k_valid = w_ref.shape[0]

    @pl.when(changed)
    def _():
        w_bf[:k_valid, :] = w_ref[...].astype(BF16)
        w_bf[k_valid:, :] = jnp.zeros((w_bf.shape[0] - k_valid, w_bf.shape[1]), BF16)

    @pl.when(active)
    def _():
        o_ref[...] = _bdot(x_ref[...], w_bf[...])

    @pl.when(jnp.logical_not(active))
    def _():
        o_ref[...] = jnp.zeros(o_ref.shape, F32)


def _moe_down(act, plan, w_down, layer):
    _, _, tile_expert, tile_block, n_used = plan
    k_valid, n = w_down.shape[2:]
    k_pad = pl.cdiv(k_valid, MOE_K_ALIGN) * MOE_K_ALIGN
    assert k_pad <= act.shape[1]
    n_tiles = tile_expert.shape[0]
    tn = MOE_DOWN_TN
    return pl.pallas_call(
        _moe_down_kernel,
        grid_spec=pltpu.PrefetchScalarGridSpec(
            num_scalar_prefetch=3, grid=(n // tn, n_tiles),
            in_specs=[pl.BlockSpec((MOE_TILE, k_pad), lambda j, i, te, tb, nu: (tb[i], 0)),
                      pl.BlockSpec((None, None, k_valid, tn), lambda j, i, te, tb, nu: (layer, te[i], 0, j))],
            out_specs=pl.BlockSpec((MOE_TILE, tn), lambda j, i, te, tb, nu: (i, j)),
            scratch_shapes=[pltpu.VMEM((k_pad, tn), BF16)]),
        out_shape=jax.ShapeDtypeStruct((act.shape[0], n), F32),
        compiler_params=_params("arbitrary", "arbitrary"),
        name="moe_down",
    )(tile_expert, tile_block, n_used, act, w_down)


def _moe_combine_kernel(pos_ref, x_ref, wt_ref, gpost_ref, gate_ref, y_hbm, *rest, tok):
    xo_refs, (buf, sem) = rest[:-2], rest[-2:]
    tm = x_ref.shape[0]
    slot = _row_gather_pipeline(pos_ref, TOP_K * tm, y_hbm, buf, sem, pl.num_programs(0))
    wt = wt_ref[...]
    f = wt[:, 0:1] * buf[slot, pl.ds(0, tm), :]
    for k in range(1, TOP_K):
        f = f + wt[:, k:k + 1] * buf[slot, pl.ds(k * tm, tm), :]
    _store_rows(tok, xo_refs, x_ref[...] + gate_ref[...] * _rms(f, gpost_ref[...]))


def _moe_combine(tok, x, y, dest, wt, g_post, mod5, layer, seg_gate, split_out=False):
    t, d = x.shape
    tm = NORM_TILE
    pos = dest.reshape(t // tm, tm, TOP_K).transpose(0, 2, 1).reshape(-1)
    row = pl.BlockSpec((tm, d), lambda i, pos: (i, 0))
    return pl.pallas_call(
        functools.partial(_moe_combine_kernel, tok=tok),
        grid_spec=pltpu.PrefetchScalarGridSpec(
            num_scalar_prefetch=1, grid=(t // tm,),
            in_specs=[row, pl.BlockSpec((tm, TOP_K), lambda i, pos: (i, 0)),
                      pl.BlockSpec((None, 1, d), lambda i, pos: (layer, 0, 0)),
                      pl.BlockSpec((None, None, None, 1, d),
                                   lambda i, pos: (layer, tok.cond_row(i, tm), seg_gate, 0, 0)),
                      pl.BlockSpec(memory_space=pl.ANY)],
            out_specs=_split_row_specs(tok, d) if split_out else [row],
            scratch_shapes=[pltpu.VMEM((2, TOP_K * tm, d), F32), pltpu.SemaphoreType.DMA((2,))]),
        out_shape=_split_shapes(tok, d) if split_out else [jax.ShapeDtypeStruct((t, d), F32)],
        compiler_params=_params("arbitrary"),
        name="moe_combine",
    )(pos, x, wt, g_post.reshape(g_post.shape[0], 1, d), mod5, y)


SEG_SHIFT_MIX, SEG_SCALE_MIX, SEG_GATE_MIX, SEG_SHIFT_FFN, SEG_SCALE_FFN, SEG_GATE_FFN = range(N_MOD)


def kernel(x_prompt, x_sample, c, c_ctx, cache_k, cache_v, state_ssm_re, state_ssm_im, w_mod, b_mod, g_mix_pre, g_mix_post, g_ffn_pre, g_ffn_post, w_in, g_q, g_k, ssm_a_re, ssm_a_im, ssm_log_dt, ssm_b_re, ssm_b_im, ssm_c_re, ssm_c_im, ssm_d, w_glu, b_glu, conv_w, w_out, w_ffn_gate, w_ffn_up, w_ffn_down, w_router, b_router, w_exp_gate, w_exp_up, w_exp_down):
    n_ctx_seq, ctx_len, d = x_prompt.shape
    n_lat_seq, lat_len, _ = x_sample.shape
    depth = w_in.shape[0]
    tok = _Tokens(n_ctx_seq, ctx_len, n_lat_seq, lat_len)
    attn_w = d // 2
    kv_w = attn_w // Q_PER_KV
    ssm_w = ssm_d.shape[-1]
    n_kv = kv_w // HEAD_DIM
    n_dir, n_grp, n_state = ssm_a_re.shape[1:]
    nb = n_grp // SSM_GROUPS_PER_BLOCK
    sw = 2 * SSM_GROUPS_PER_BLOCK * n_state

    x = (x_prompt.reshape(tok.n_ctx, d), x_sample.reshape(tok.n_lat, d))
    cond = jnp.concatenate([c_ctx[None, :], c, jnp.zeros((8 - 1 - n_lat_seq, d), F32)], axis=0)
    mod = _modulation(cond, w_mod, b_mod)
    mod5 = mod.reshape(depth, cond.shape[0], N_MOD, 1, d)
    cos, sin = _rope_tables(tok)
    ck = cache_k.reshape(cache_k.shape[:3] + (kv_w,))
    cv = cache_v.reshape(cache_v.shape[:3] + (kv_w,))

    ssm_tables = jax.vmap(_ssm_tables)(ssm_a_re, ssm_a_im, ssm_log_dt, ssm_b_re, ssm_b_im, ssm_c_re, ssm_c_im)

    h = _prenorm(tok, x, g_mix_pre, mod5, 0, SEG_SHIFT_MIX, SEG_SCALE_MIX)
    new_k, new_v, new_re, new_im = [], [], [], []
    for l in range(depth):
        proj = _matmul(tok, [(h, 0)], w_in, l, F32)
        qn, kf, kb, vb = _qkv_prep(tok, proj, g_q, g_k, l, attn_w, kv_w, cos, sin)
        attn = (_attention_ctx(tok, qn, kb, vb), _attention_lat(tok, qn, kb, vb, ck, cv, l))
        new_k.append(kf[:tok.n_ctx].reshape(n_ctx_seq, ctx_len, n_kv, HEAD_DIM))
        new_v.append(proj[:tok.n_ctx, attn_w + kv_w:attn_w + 2 * kv_w].reshape(n_ctx_seq, ctx_len, n_kv, HEAD_DIM))

        u_col = attn_w + 2 * kv_w
        h0 = jnp.concatenate([state_ssm_re[:, l].reshape(n_lat_seq, n_dir, nb, sw // 2),
                              state_ssm_im[:, l].reshape(n_lat_seq, n_dir, nb, sw // 2)], axis=-1)
        ydir, fin = _ssm_scan(tok, proj, u_col // LANES, ssm_tables, l, h0.transpose(1, 2, 0, 3))
        fin = fin.transpose(2, 0, 1, 3).reshape(n_ctx_seq, n_dir, nb, 2, SSM_GROUPS_PER_BLOCK, n_state)
        new_re.append(fin[:, :, :, 0].reshape(n_ctx_seq, n_dir, n_grp, n_state))
        new_im.append(fin[:, :, :, 1].reshape(n_ctx_seq, n_dir, n_grp, n_state))
        ssm_out = _ssm_glu(proj, u_col // ssm_w, ydir, ssm_d, w_glu, b_glu, l)

        conv_out = _conv(tok, proj, (u_col + ssm_w) // ssm_w, conv_w, l)
        mix = _matmul(tok, [(attn, 0), (ssm_out, attn_w // ssm_w), (conv_out, attn_w // ssm_w + 1)],
                      w_out, l, F32)
        dense = l % 2 == 0
        x, h, *routing = _resid_norm(
            tok, x, mix, g_mix_post, mod5, l, SEG_GATE_MIX, nxt=(g_ffn_pre, l, SEG_SHIFT_FFN, SEG_SCALE_FFN),
            h_dtype=BF16 if dense else jnp.uint32,
            router=None if dense else (jnp.swapaxes(w_router, 1, 2), b_router, l // 2))
        last = l + 1 == depth
        nxt = None if last else (g_mix_pre, l + 1, SEG_SHIFT_MIX, SEG_SCALE_MIX)
        if dense:
            act = _swiglu_up(h, w_ffn_gate, w_ffn_up, l // 2)
            f = _down(act, w_ffn_down.reshape((w_ffn_down.shape[0], 1) + w_ffn_down.shape[1:]), l // 2)
            outs = _resid_norm(tok, x, f, g_ffn_post, mod5, l, SEG_GATE_FFN, nxt=nxt, split_out=last)
            x, h = (tuple(outs), None) if last else outs
        else:
            m = l // 2
            idx, wt = routing
            plan = _moe_plan(idx, w_router.shape[-1])
            xs = _gather_rows(h, plan[1], plan[4] * MOE_TILE)
            act = _moe_up(xs, plan, jnp.swapaxes(w_exp_gate, 2, 3), jnp.swapaxes(w_exp_up, 2, 3), m)
            y = _moe_down(act, plan, w_exp_down, m)
            outs = _moe_combine(tok, x, y, plan[0], wt, g_ffn_post, mod5, l, SEG_GATE_FFN, split_out=last)
            if last:
                x = tuple(outs)
            else:
                (x,) = outs
                h = _prenorm(tok, x, nxt[0], mod5, nxt[1], nxt[2], nxt[3])

    return (x[0].reshape(x_prompt.shape), x[1].reshape(x_sample.shape),
            jnp.stack(new_k, axis=1), jnp.stack(new_v, axis=1),
            jnp.stack(new_re, axis=1), jnp.stack(new_im, axis=1))
```

```python
import functools
import math

import jax
import jax.numpy as jnp
import numpy as np
from jax import lax
from jax.experimental import pallas as pl
from jax.experimental.pallas import tpu as pltpu

F32 = jnp.float32
BF16 = jnp.bfloat16

HEAD_DIM = 128
Q_PER_KV = 4
GRID_W = 64
SSM_GROUP = 16
N_MOD = 6
TOP_K = 2
ROPE_THETA = 10000.0
EPS = 1e-6

LANES = 128
MXU_DIM = 256
VMEM_LIMIT_BYTES = 56 * 1024 * 1024

SSM_CHUNK = 8
SSM_SCAN_UNROLL = 4
SSM_GROUPS_PER_BLOCK = LANES // SSM_GROUP


def _params(*semantics):
    return pltpu.CompilerParams(dimension_semantics=semantics, vmem_limit_bytes=VMEM_LIMIT_BYTES)


def _rms(x, g):
    return x * lax.rsqrt(jnp.mean(x * x, axis=-1, keepdims=True) + EPS) * g


def _silu(x):
    return x * jax.nn.sigmoid(x)


def _bdot(a, b):
    return jnp.dot(a, b, preferred_element_type=F32)


def _pack_bf16_pairs(x):
    half = x.shape[1] // 2
    lo = lax.bitcast_convert_type(x[:, :half].astype(BF16).astype(F32), jnp.uint32)
    hi = lax.bitcast_convert_type(x[:, half:].astype(BF16).astype(F32), jnp.uint32)
    return (lo >> 16) | (hi & jnp.uint32(0xFFFF0000))


def _unpack_bf16_pairs(p):
    lo = lax.bitcast_convert_type(p << 16, F32)
    hi = lax.bitcast_convert_type(p & jnp.uint32(0xFFFF0000), F32)
    return jnp.concatenate([lo, hi], axis=1).astype(BF16)


class _Tokens:
    def __init__(self, n_ctx_seq, ctx_len, n_lat_seq, lat_len):
        self.n_ctx_seq, self.ctx_len = n_ctx_seq, ctx_len
        self.n_lat_seq, self.lat_len = n_lat_seq, lat_len
        self.n_ctx = n_ctx_seq * ctx_len
        self.n_lat = n_lat_seq * lat_len
        self.total = self.n_ctx + self.n_lat

    def cond_row(self, i, tile):
        ctx_tiles = self.n_ctx // tile
        lat_tiles = self.lat_len // tile
        return jnp.where(i < ctx_tiles, 0, 1 + (i - ctx_tiles) // lat_tiles)


def _mod_kernel(c_ref, w_ref, b_ref, o_ref):
    s = _silu(c_ref[...]).astype(BF16)
    o_ref[...] = _bdot(s, w_ref[...].astype(BF16)) + b_ref[...]


def _modulation(cond, w_mod, b_mod):
    depth, d, n = w_mod.shape
    rows = cond.shape[0]
    tn = 512
    return pl.pallas_call(
        _mod_kernel,
        grid=(depth, n // tn),
        in_specs=[
            pl.BlockSpec((rows, d), lambda l, j: (0, 0)),
            pl.BlockSpec((None, d, tn), lambda l, j: (l, 0, j)),
            pl.BlockSpec((None, 1, tn), lambda l, j: (l, 0, j)),
        ],
        out_specs=pl.BlockSpec((None, rows, tn), lambda l, j: (l, 0, j)),
        out_shape=jax.ShapeDtypeStruct((depth, rows, n), F32),
        compiler_params=_params("parallel", "parallel"),
        name="modulation",
    )(cond, w_mod, b_mod.reshape(depth, 1, n))


NORM_TILE = 256


def _mod_spec(tok, layer, seg, d):
    return pl.BlockSpec((None, None, None, 1, d),
                        lambda i: (layer, tok.cond_row(i, NORM_TILE), seg, 0, 0))


def _split_row_specs(tok, d):
    ct = tok.n_ctx // NORM_TILE
    return [pl.BlockSpec((NORM_TILE, d), lambda i, *_: (jnp.minimum(i, ct - 1), 0)),
            pl.BlockSpec((NORM_TILE, d), lambda i, *_: (jnp.maximum(i - ct, 0), 0))]


def _split_shapes(tok, d):
    return [jax.ShapeDtypeStruct((tok.n_ctx, d), F32), jax.ShapeDtypeStruct((tok.n_lat, d), F32)]


def _load_rows(tok, refs):
    if len(refs) == 1:
        return refs[0][...]
    return jnp.where(pl.program_id(0) < tok.n_ctx // NORM_TILE, refs[0][...], refs[1][...])


def _store_rows(tok, refs, val):
    if len(refs) == 1:
        refs[0][...] = val
        return
    is_ctx = pl.program_id(0) < tok.n_ctx // NORM_TILE

    @pl.when(is_ctx)
    def _():
        refs[0][...] = val

    @pl.when(jnp.logical_not(is_ctx))
    def _():
        refs[1][...] = val


def _as_rows(x):
    return list(x) if isinstance(x, (tuple, list)) else [x]


def _prenorm_kernel(*refs, tok, n_x):
    g_ref, sc_ref, sh_ref, h_ref = refs[n_x:]
    h = _rms(_load_rows(tok, refs[:n_x]), g_ref[...]) * (1.0 + sc_ref[...]) + sh_ref[...]
    h_ref[...] = h.astype(BF16)


def _prenorm(tok, x, g, mod5, layer, seg_shift, seg_scale):
    xs = _as_rows(x)
    d = xs[0].shape[1]
    row = pl.BlockSpec((NORM_TILE, d), lambda i: (i, 0))
    return pl.pallas_call(
        functools.partial(_prenorm_kernel, tok=tok, n_x=len(xs)),
        grid=(tok.total // NORM_TILE,),
        in_specs=([row] if len(xs) == 1 else _split_row_specs(tok, d))
        + [pl.BlockSpec((None, 1, d), lambda i: (layer, 0, 0)),
           _mod_spec(tok, layer, seg_scale, d), _mod_spec(tok, layer, seg_shift, d)],
        out_specs=row,
        out_shape=jax.ShapeDtypeStruct((tok.total, d), BF16),
        compiler_params=_params("arbitrary"),
        name="prenorm",
    )(*xs, g.reshape(g.shape[0], 1, d), mod5, mod5)


def _resid_norm_kernel(*refs, tok, n_x, n_xo, with_next, with_router):
    y_ref, gpost_ref, gate_ref = refs[n_x:n_x + 3]
    rest = list(refs[n_x + 3:])
    x = _load_rows(tok, refs[:n_x]) + gate_ref[...] * _rms(y_ref[...], gpost_ref[...])
    if with_next:
        gpre_ref, sc_ref, sh_ref = rest[:3]
        rest = rest[3:]
        h = _rms(x, gpre_ref[...]) * (1.0 + sc_ref[...]) + sh_ref[...]
        if with_router:
            wr_ref, br_ref = rest[:2]
            rest = rest[2:]
            idx_ref, wt_ref = rest[n_xo + 1:]
            idx_ref[...], wt_ref[...] = _route(h, wr_ref[...], br_ref[...])
        h_ref = rest[n_xo]
        h_ref[...] = _pack_bf16_pairs(h) if h_ref.dtype == jnp.uint32 else h.astype(h_ref.dtype)
    _store_rows(tok, rest[:n_xo], x)


def _resid_norm(tok, x, y, g_post, mod5, layer, seg_gate, nxt=None, h_dtype=BF16, split_out=False, router=None):
    xs = _as_rows(x)
    t, d = y.shape
    row = pl.BlockSpec((NORM_TILE, d), lambda i: (i, 0))
    in_specs = ([row] if len(xs) == 1 else _split_row_specs(tok, d)) + [
        row, pl.BlockSpec((None, 1, d), lambda i: (layer, 0, 0)), _mod_spec(tok, layer, seg_gate, d)]
    args = xs + [y, g_post.reshape(g_post.shape[0], 1, d), mod5]
    out_specs = _split_row_specs(tok, d) if split_out else [row]
    out_shape = _split_shapes(tok, d) if split_out else [jax.ShapeDtypeStruct((t, d), F32)]
    n_xo = len(out_specs)
    if nxt is not None:
        g_pre, nl, seg_shift, seg_scale = nxt
        in_specs += [pl.BlockSpec((None, 1, d), lambda i: (nl, 0, 0)),
                     _mod_spec(tok, nl, seg_scale, d), _mod_spec(tok, nl, seg_shift, d)]
        args += [g_pre.reshape(g_pre.shape[0], 1, d), mod5, mod5]
        hw = d // 2 if h_dtype == jnp.uint32 else d
        out_specs.append(pl.BlockSpec((NORM_TILE, hw), lambda i: (i, 0)))
        out_shape.append(jax.ShapeDtypeStruct((t, hw), h_dtype))
    if router is not None:
        w_router_t, b_router, rl = router
        n_exp = w_router_t.shape[1]
        in_specs += [pl.BlockSpec((None, n_exp, d), lambda i: (rl, 0, 0)),
                     pl.BlockSpec((None, 1, n_exp), lambda i: (rl, 0, 0))]
        args += [w_router_t, b_router.reshape(-1, 1, n_exp)]
        out_specs += [pl.BlockSpec((NORM_TILE, TOP_K), lambda i: (i, 0))] * 2
        out_shape += [jax.ShapeDtypeStruct((t, TOP_K), jnp.int32), jax.ShapeDtypeStruct((t, TOP_K), F32)]
    return pl.pallas_call(
        functools.partial(_resid_norm_kernel, tok=tok, n_x=len(xs), n_xo=n_xo, with_next=nxt is not None,
                          with_router=router is not None),
        grid=(t // NORM_TILE,),
        in_specs=in_specs, out_specs=out_specs, out_shape=out_shape,
        compiler_params=_params("arbitrary"),
        name="resid_norm",
    )(*args)


MM_TM = 1024
MM_TN = 512


def _mm_kernel(*refs, n_parts, ctx_tiles):
    o_ref = refs[-1]
    n_x = sum(n_parts)
    acc, at = None, 0
    for s, parts in enumerate(n_parts):
        x = refs[at][...]
        if parts == 2:
            x = jnp.where(pl.program_id(0) < ctx_tiles, x, refs[at + 1][...])
        at += parts
        p = _bdot(x, refs[n_x + s][...].astype(BF16))
        acc = p if acc is None else acc + p
    o_ref[...] = acc.astype(o_ref.dtype)


def _matmul(tok, segments, w, layer, out_dtype, tn=MM_TN):
    m = tok.total
    n = w.shape[-1]
    ct = tok.n_ctx // MM_TM
    in_specs, args, n_parts = [], [], []
    for x, _ in segments:
        xs = _as_rows(x)
        k = xs[0].shape[1]
        if len(xs) == 1:
            in_specs.append(pl.BlockSpec((MM_TM, k), lambda i, j: (i, 0)))
        else:
            in_specs += [pl.BlockSpec((MM_TM, k), lambda i, j: (jnp.minimum(i, ct - 1), 0)),
                         pl.BlockSpec((MM_TM, k), lambda i, j: (jnp.maximum(i - ct, 0), 0))]
        args += xs
        n_parts.append(len(xs))
    for x, rb in segments:
        k = _as_rows(x)[0].shape[1]
        in_specs.append(pl.BlockSpec((None, k, tn), lambda i, j, rb=rb: (layer, rb, j)))
        args.append(w)
    return pl.pallas_call(
        functools.partial(_mm_kernel, n_parts=tuple(n_parts), ctx_tiles=ct),
        grid=(m // MM_TM, n // tn),
        in_specs=in_specs,
        out_specs=pl.BlockSpec((MM_TM, tn), lambda i, j: (i, j)),
        out_shape=jax.ShapeDtypeStruct((m, n), out_dtype),
        compiler_params=_params("parallel", "parallel"),
        name="matmul",
    )(*args)


def _swiglu_up_kernel(x_ref, wg_ref, wu_ref, o_ref):
    x = x_ref[...]
    g = _bdot(x, wg_ref[...].astype(BF16))
    u = _bdot(x, wu_ref[...].astype(BF16))
    o_ref[...] = (_silu(g) * u).astype(BF16)


def _swiglu_up(h, w_gate, w_up, layer):
    m, d = h.shape
    n = w_gate.shape[-1]
    tn = 256
    wspec = pl.BlockSpec((None, d, tn), lambda i, j: (layer, 0, j))
    return pl.pallas_call(
        _swiglu_up_kernel,
        grid=(m // MM_TM, n // tn),
        in_specs=[pl.BlockSpec((MM_TM, d), lambda i, j: (i, 0)), wspec, wspec],
        out_specs=pl.BlockSpec((MM_TM, tn), lambda i, j: (i, j)),
        out_shape=jax.ShapeDtypeStruct((m, n), BF16),
        compiler_params=_params("parallel", "parallel"),
        name="swiglu_up",
    )(h, w_gate, w_up)


DOWN_TK = 512


def _down_kernel(x_ref, w_ref, o_ref, *, k_valid):
    k = pl.program_id(1)

    @pl.when(k == 0)
    def _():
        o_ref[...] = jnp.zeros(o_ref.shape, F32)

    x = x_ref[...]
    ragged = k_valid % DOWN_TK != 0
    if ragged:
        col = k * DOWN_TK + lax.broadcasted_iota(jnp.int32, (1, DOWN_TK), 1)
        x = jnp.where(col < k_valid, x, jnp.zeros_like(x))
        row_ok = k * DOWN_TK + lax.broadcasted_iota(jnp.int32, (DOWN_TK, 1), 0) < k_valid
    for c in range(0, o_ref.shape[1], MM_TN):
        w = w_ref[:, c:c + MM_TN]
        if ragged:
            w = jnp.where(row_ok, w, 0.0)
        o_ref[:, c:c + MM_TN] += _bdot(x, w.astype(BF16))


def _down(act, w, layer):
    m, k_valid = act.shape
    n = w.shape[-1]
    return pl.pallas_call(
        functools.partial(_down_kernel, k_valid=k_valid),
        grid=(m // MM_TM, pl.cdiv(k_valid, DOWN_TK)),
        in_specs=[pl.BlockSpec((MM_TM, DOWN_TK), lambda i, k: (i, k)),
                  pl.BlockSpec((None, DOWN_TK, n), lambda i, k: (layer, k, 0))],
        out_specs=pl.BlockSpec((MM_TM, n), lambda i, k: (i, 0)),
        out_shape=jax.ShapeDtypeStruct((m, n), F32),
        compiler_params=_params("parallel", "arbitrary"),
        name="down",
    )(act, w)


ATTN_TILE = 256
SOFTMAX_Q_SCALE = HEAD_DIM ** -0.5 * math.log2(math.e)
CTX_HEADS_PER_CHAIN = 2
LAT_HEADS_PER_CHAIN = 1


def _rope_tables(tok):
    pos = np.arange(tok.lat_len)
    half = HEAD_DIM // 2
    inv = ROPE_THETA ** (-np.arange(0, half, 2, dtype=np.float64) / half)
    ang_r = (pos // GRID_W)[:, None] * inv[None, :]
    ang_c = (pos % GRID_W)[:, None] * inv[None, :]
    cos = np.concatenate([np.cos(ang_r)] * 2 + [np.cos(ang_c)] * 2, axis=-1)
    sin = np.concatenate([-np.sin(ang_r), np.sin(ang_r), -np.sin(ang_c), np.sin(ang_c)], axis=-1)
    ident = np.ones((ATTN_TILE, HEAD_DIM), np.float32)
    return (jnp.asarray(np.concatenate([ident, cos], axis=0), F32),
            jnp.asarray(np.concatenate([0.0 * ident, sin], axis=0), F32))


def _qkv_prep_kernel(q_ref, k_ref, v_ref, gq_ref, gk_ref, cos_ref, sin_ref,
                     qo_ref, kf_ref, kb_ref, vb_ref, *, ctx_tiles):
    def run(rope):
        if rope:
            cos, sin = cos_ref[...], sin_ref[...]
            lane = lax.broadcasted_iota(jnp.int32, cos.shape, 1)
            first = (lane % (HEAD_DIM // 2)) < (HEAD_DIM // 4)
            quarter = HEAD_DIM // 4

        def norm_rope(x, g):
            xn = _rms(x, g)
            if not rope:
                return xn
            partner = jnp.where(first, pltpu.roll(xn, HEAD_DIM - quarter, 1), pltpu.roll(xn, quarter, 1))
            return xn * cos + partner * sin

        gq, gk = gq_ref[...], gk_ref[...]
        for h in range(q_ref.shape[1] // HEAD_DIM):
            sl = slice(h * HEAD_DIM, (h + 1) * HEAD_DIM)
            qo_ref[:, sl] = (norm_rope(q_ref[:, sl], gq) * SOFTMAX_Q_SCALE).astype(BF16)
        for h in range(k_ref.shape[1] // HEAD_DIM):
            sl = slice(h * HEAD_DIM, (h + 1) * HEAD_DIM)
            kn = norm_rope(k_ref[:, sl], gk)
            kf_ref[:, sl] = kn
            kb_ref[:, sl] = kn.astype(BF16)

    is_ctx = pl.program_id(0) < ctx_tiles
    pl.when(is_ctx)(functools.partial(run, False))
    pl.when(jnp.logical_not(is_ctx))(functools.partial(run, True))
    vb_ref[...] = v_ref[...].astype(BF16)


def _qkv_prep(tok, proj, g_q, g_k, layer, attn_w, kv_w, cos, sin):
    t = proj.shape[0]
    tm = ATTN_TILE
    ctx_tiles = tok.n_ctx // tm
    lat_tiles = tok.lat_len // tm

    def table_row(i):
        return jnp.where(i < ctx_tiles, 0, 1 + (i - ctx_tiles) % lat_tiles)

    kv_blk = attn_w // kv_w
    gspec = pl.BlockSpec((None, 1, HEAD_DIM), lambda i: (layer, 0, 0))
    tspec = pl.BlockSpec((tm, HEAD_DIM), lambda i: (table_row(i), 0))
    kvspec = pl.BlockSpec((tm, kv_w), lambda i: (i, 0))
    return pl.pallas_call(
        functools.partial(_qkv_prep_kernel, ctx_tiles=ctx_tiles),
        grid=(t // tm,),
        in_specs=[pl.BlockSpec((tm, attn_w), lambda i: (i, 0)),
                  pl.BlockSpec((tm, kv_w), lambda i: (i, kv_blk)),
                  pl.BlockSpec((tm, kv_w), lambda i: (i, kv_blk + 1)),
                  gspec, gspec, tspec, tspec],
        out_specs=[pl.BlockSpec((tm, attn_w), lambda i: (i, 0)), kvspec, kvspec, kvspec],
        out_shape=[jax.ShapeDtypeStruct((t, attn_w), BF16), jax.ShapeDtypeStruct((t, kv_w), F32),
                   jax.ShapeDtypeStruct((t, kv_w), BF16), jax.ShapeDtypeStruct((t, kv_w), BF16)],
        compiler_params=_params("parallel"),
        name="qkv_prep",
    )(proj, proj, proj, g_q.reshape(-1, 1, HEAD_DIM), g_k.reshape(-1, 1, HEAD_DIM), cos, sin)


def _attn_kernel(q_ref, *refs, n_kv, heads_per_chain):
    o_ref = refs[-1]
    tq = q_ref.shape[0]
    ks = [refs[2 * p][...].astype(BF16) for p in range(n_kv)]
    vs = [refs[2 * p + 1][...].astype(BF16) for p in range(n_kv)]
    for g0 in range(0, Q_PER_KV, heads_per_chain):
        heads = range(g0, g0 + heads_per_chain)
        q = jnp.concatenate([q_ref[:, g * HEAD_DIM:(g + 1) * HEAD_DIM] for g in heads], axis=0)
        ss = [lax.dot_general(q, k, (((1,), (1,)), ((), ())), preferred_element_type=F32) for k in ks]
        m = functools.reduce(jnp.maximum, [jnp.max(s, axis=-1, keepdims=True) for s in ss])
        ps = [jnp.exp2(s - m) for s in ss]
        denom = functools.reduce(jnp.add, [jnp.sum(p, axis=-1, keepdims=True) for p in ps])
        o = functools.reduce(jnp.add, [_bdot(p.astype(BF16), v) for p, v in zip(ps, vs)])
        o = o / denom
        for n, g in enumerate(heads):
            o_ref[:, g * HEAD_DIM:(g + 1) * HEAD_DIM] = o[n * tq:(n + 1) * tq].astype(BF16)


def _attention_ctx(tok, qn, kb, vb):
    L = tok.ctx_len
    n_kv = kb.shape[1] // HEAD_DIM
    qw = Q_PER_KV * HEAD_DIM
    kvspec = pl.BlockSpec((L, HEAD_DIM), lambda s, h: (s, h))
    qspec = pl.BlockSpec((L, qw), lambda s, h: (s, h))
    return pl.pallas_call(
        functools.partial(_attn_kernel, n_kv=1, heads_per_chain=CTX_HEADS_PER_CHAIN),
        grid=(tok.n_ctx_seq, n_kv),
        in_specs=[qspec, kvspec, kvspec],
        out_specs=qspec,
        out_shape=jax.ShapeDtypeStruct((tok.n_ctx, qn.shape[1]), BF16),
        compiler_params=_params("parallel", "parallel"),
        name="attn_ctx",
    )(qn, kb, vb)


def _attention_lat(tok, qn, kb, vb, cache_k, cache_v, layer):
    L = tok.lat_len
    n_kv = kb.shape[1] // HEAD_DIM
    qw = Q_PER_KV * HEAD_DIM
    tq = ATTN_TILE
    q0 = tok.n_ctx // tq
    k0 = tok.n_ctx // L
    past = cache_k.shape[2]
    cspec = pl.BlockSpec((None, None, past, HEAD_DIM), lambda b, h, i: (b, layer, 0, h))
    kvspec = pl.BlockSpec((L, HEAD_DIM), lambda b, h, i: (k0 + b, h))
    return pl.pallas_call(
        functools.partial(_attn_kernel, n_kv=2, heads_per_chain=LAT_HEADS_PER_CHAIN),
        grid=(tok.n_lat_seq, n_kv, L // tq),
        in_specs=[pl.BlockSpec((tq, qw), lambda b, h, i: (q0 + b * (L // tq) + i, h)),
                  cspec, cspec, kvspec, kvspec],
        out_specs=pl.BlockSpec((tq, qw), lambda b, h, i: (b * (L // tq) + i, h)),
        out_shape=jax.ShapeDtypeStruct((tok.n_lat, qn.shape[1]), BF16),
        compiler_params=_params("parallel", "parallel", "parallel"),
        name="attn_lat",
    )(qn, cache_k, cache_v, kb, vb)


def _ssm_tables(a_re, a_im, log_dt, b_re, b_im, c_re, c_im):
    hi = lax.Precision.HIGHEST
    n_dir, G, P = a_re.shape
    Q, gb = SSM_CHUNK, SSM_GROUPS_PER_BLOCK
    nb = G // gb
    dt = jnp.exp(log_dt)[None, :, :, None]
    ks = jnp.arange(Q + 1, dtype=F32)[:, None, None, None]
    mag = jnp.exp(ks * dt * a_re[None])
    ang = ks * dt * a_im[None]
    pr, pi = mag * jnp.cos(ang), mag * jnp.sin(ang)
    nr, ni = pr[1] - 1.0, pi[1]
    den = a_re * a_re + a_im * a_im
    qr, qi = (nr * a_re + ni * a_im) / den, (ni * a_re - nr * a_im) / den
    bt_re, bt_im = jnp.swapaxes(b_re, -1, -2), jnp.swapaxes(b_im, -1, -2)
    bbr = qr[:, :, None, :] * bt_re - qi[:, :, None, :] * bt_im
    bbi = qr[:, :, None, :] * bt_im + qi[:, :, None, :] * bt_re
    pkr, pki = pr[:Q, :, :, None, :], pi[:Q, :, :, None, :]
    pbr = pkr * bbr[None] - pki * bbi[None]
    pbi = pkr * bbi[None] + pki * bbr[None]
    pcr, pci = pr[1:, :, :, None, :], pi[1:, :, :, None, :]
    car = c_re[None] * pcr - c_im[None] * pci
    cai = c_re[None] * pci + c_im[None] * pcr

    def rows128(re, im):
        t = jnp.concatenate([re, im], axis=-1)
        return t.reshape(t.shape[:-3] + (nb, LANES, 2 * P))

    def lead(t):
        return jnp.transpose(t, (1, 2, 0) + tuple(range(3, t.ndim)))

    e1 = rows128(pbr, pbi)
    kt = jnp.einsum('kdjan,djbn->kdjab', e1, rows128(c_re, -c_im), precision=hi)
    a_q = jnp.concatenate([pr[Q].reshape(n_dir, nb, 1, gb * P), pi[Q].reshape(n_dir, nb, 1, gb * P)], axis=-1)
    return lead(kt), lead(e1), lead(rows128(car, -cai)), a_q


def _ssm_kernel(u_ref, kt_ref, e1_ref, ca_ref, aq_ref, h0_ref, y_ref, fin_ref,
                we_scr, wy_scr, s_scr, hin_scr, *, tok):
    Q = SSM_CHUNK
    rows = u_ref.shape[0] // Q
    nlb = s_scr.shape[0]
    half = nlb // 2
    sw = nlb * LANES
    n_state = e1_ref.shape[2] // 2
    reverse = pl.program_id(0) == 1

    def lanes(b):
        return slice(b * LANES, (b + 1) * LANES)

    def iota(shape, dim):
        return lax.broadcasted_iota(jnp.int32, shape, dim)

    spread = jnp.where(jnp.logical_and(iota((LANES, sw), 0) // n_state == iota((LANES, sw), 1) // (sw // 2),
                                       iota((LANES, sw), 0) % n_state == iota((LANES, sw), 1) % n_state),
                       1.0, 0.0).astype(BF16)
    own_state = iota((LANES, sw), 0) // SSM_GROUP == (iota((LANES, sw), 1) % (sw // 2)) // n_state
    for s in range(Q):
        e1 = e1_ref[jnp.where(reverse, s, Q - 1 - s)].astype(BF16)
        we_scr[lanes(s), :] = jnp.where(own_state, _bdot(e1, spread), 0.0).astype(BF16)
    own_group = iota((LANES, LANES), 0) // SSM_GROUP == iota((LANES, LANES), 1) // SSM_GROUP
    for s in range(Q):
        for t in range(Q):
            if s == t:
                keep = own_group
            elif t > s:
                keep = jnp.logical_and(own_group, jnp.logical_not(reverse))
            else:
                keep = jnp.logical_and(own_group, reverse)
            wy_scr[lanes(s), lanes(t)] = jnp.where(keep, kt_ref[abs(t - s)], 0.0).astype(BF16)
    own_state_t = (iota((sw, LANES), 0) % (sw // 2)) // n_state == iota((sw, LANES), 1) // SSM_GROUP
    n_rep = sw // 2 // n_state
    for t in range(Q):
        fc = ca_ref[jnp.where(reverse, Q - 1 - t, t)].T
        tiled = jnp.concatenate([fc[:n_state]] * n_rep + [fc[n_state:]] * n_rep, axis=0)
        wy_scr[Q * LANES:, lanes(t)] = jnp.where(own_state_t, tiled, 0.0).astype(BF16)

    n_ctx_rows = tok.n_ctx // Q
    ctx_chunks, ns = tok.ctx_len // Q, tok.n_ctx_seq

    def ctx_tokens(c, t):
        return pl.ds(c * Q + t, ns, stride=tok.ctx_len)

    lat_tokens = [pl.ds(tok.n_ctx + t, rows - n_ctx_rows, stride=Q) for t in range(Q)]
    x = jnp.concatenate(
        [jnp.concatenate([u_ref[ctx_tokens(c, t), :] for c in range(ctx_chunks)] + [u_ref[lat_tokens[t], :]],
                         axis=0).astype(BF16) for t in range(Q)], axis=1)
    s = _bdot(x, we_scr[...])
    for b in range(nlb):
        s_scr[b] = s[:, lanes(b)]
    a = [aq_ref[:, lanes(b)] for b in range(nlb)]

    def scan(n_seq, n_chunk, row0, chunk_major, h_init):
        def step(i, h):
            c = jnp.where(reverse, n_chunk - 1 - i, i)
            if chunk_major:
                sel = pl.ds(pl.multiple_of(row0 + c * n_seq, n_seq), n_seq)
            else:
                sel = pl.ds(row0 + c, n_seq, stride=n_chunk)
            new = [None] * nlb
            for b in range(half):
                hr, hi = h[b], h[half + b]
                hin_scr[b, sel, :] = hr
                hin_scr[half + b, sel, :] = hi
                new[b] = a[b] * hr - a[half + b] * hi + s_scr[b, sel, :]
                new[half + b] = a[b] * hi + a[half + b] * hr + s_scr[half + b, sel, :]
            return tuple(new)
        return lax.fori_loop(0, n_chunk, step, h_init, unroll=SSM_SCAN_UNROLL)

    zeros = tuple(jnp.zeros((ns, LANES), F32) for _ in range(nlb))
    fin = scan(ns, ctx_chunks, 0, True, zeros)
    for b in range(nlb):
        fin_ref[:, lanes(b)] = fin[b]
    scan(tok.n_lat_seq, tok.lat_len // Q, n_ctx_rows, False, tuple(h0_ref[:, lanes(b)] for b in range(nlb)))
    xin = jnp.concatenate([x] + [hin_scr[b].astype(BF16) for b in range(nlb)], axis=1)
    y = _bdot(xin, wy_scr[...])
    for t in range(Q):
        for c in range(ctx_chunks):
            y_ref[ctx_tokens(c, t), :] = y[c * ns:(c + 1) * ns, lanes(t)]
        y_ref[lat_tokens[t], :] = y[n_ctx_rows:, lanes(t)]


def _ssm_scan(tok, proj, col_block0, tables, layer, h0):
    n_dir, nb = h0.shape[:2]
    t = proj.shape[0]
    Q = SSM_CHUNK
    rows = t // Q
    sw = h0.shape[-1]

    def table(x):
        return pl.BlockSpec((None, None, None) + x.shape[3:], lambda d, j: (layer, d, j) + (0,) * (x.ndim - 3))

    return pl.pallas_call(
        functools.partial(_ssm_kernel, tok=tok),
        grid=(n_dir, nb),
        in_specs=[pl.BlockSpec((t, LANES), lambda d, j: (0, col_block0 + j))] + [table(x) for x in tables]
        + [pl.BlockSpec((None, None) + h0.shape[2:], lambda d, j: (d, j, 0, 0))],
        out_specs=[pl.BlockSpec((None, t, LANES), lambda d, j: (d, 0, j)),
                   pl.BlockSpec((None, None, tok.n_ctx_seq, sw), lambda d, j: (d, j, 0, 0))],
        out_shape=[jax.ShapeDtypeStruct((n_dir, t, nb * LANES), F32),
                   jax.ShapeDtypeStruct((n_dir, nb, tok.n_ctx_seq, sw), F32)],
        scratch_shapes=[pltpu.VMEM((Q * LANES, sw), BF16), pltpu.VMEM((Q * LANES + sw, Q * LANES), BF16),
                        pltpu.VMEM((sw // LANES, rows, LANES), F32), pltpu.VMEM((sw // LANES, rows, LANES), F32)],
        compiler_params=_params("arbitrary", "arbitrary"),
        name="ssm_scan",
    )(proj, *tables, h0)


def _ssm_glu_kernel(u_ref, yf_ref, yb_ref, d_ref, w_ref, b_ref, o_ref):
    y = d_ref[...] * u_ref[...]
    y = y + yf_ref[...]
    y = y + yb_ref[...]
    z = jax.nn.gelu(y)
    gate = jax.nn.sigmoid(_bdot(z.astype(BF16), w_ref[...].astype(BF16)) + b_ref[...])
    o_ref[...] = (z * gate).astype(BF16)


def _ssm_glu(proj, col_block, ydir, ssm_d, w_glu, b_glu, layer):
    t = proj.shape[0]
    w = ydir.shape[-1]
    tm = 512
    row = pl.BlockSpec((None, 1, w), lambda i: (layer, 0, 0))
    return pl.pallas_call(
        _ssm_glu_kernel,
        grid=(t // tm,),
        in_specs=[pl.BlockSpec((tm, w), lambda i: (i, col_block)),
                  pl.BlockSpec((None, tm, w), lambda i: (0, i, 0)),
                  pl.BlockSpec((None, tm, w), lambda i: (1, i, 0)),
                  row, pl.BlockSpec((None, w, w), lambda i: (layer, 0, 0)), row],
        out_specs=pl.BlockSpec((tm, w), lambda i: (i, 0)),
        out_shape=jax.ShapeDtypeStruct((t, w), BF16),
        compiler_params=_params("parallel"),
        name="ssm_glu",
    )(proj, ydir, ydir, ssm_d.reshape(-1, 1, w), w_glu, b_glu.reshape(-1, 1, w))


CONV_TILE = 256
CONV_HALO = 8


def _conv_kernel(cx_ref, cb_ref, cc_ref, pcx_ref, pcc_ref, ncx_ref, ncc_ref, w_ref, o_ref, *, tok):
    i = pl.program_id(0)
    tm = CONV_TILE
    tiles_per_lat = tok.lat_len // tm
    ctx_tiles = tok.n_ctx // tm
    j = (i - ctx_tiles) % tiles_per_lat
    is_ctx = i < ctx_tiles
    has_prev = jnp.logical_and(jnp.logical_not(is_ctx), j > 0)
    has_next = jnp.logical_and(jnp.logical_not(is_ctx), j < tiles_per_lat - 1)
    z = cc_ref[...] * cx_ref[...]
    z_prev = jnp.where(has_prev, pcc_ref[CONV_HALO - 1:, :] * pcx_ref[CONV_HALO - 1:, :], 0.0)
    z_next = jnp.where(has_next, ncc_ref[:1, :] * ncx_ref[:1, :], 0.0)
    row = lax.broadcasted_iota(jnp.int32, z.shape, 0)
    up = jnp.where(row == 0, z_prev, pltpu.roll(z, 1, 0))
    dn = jnp.where(row == tm - 1, z_next, pltpu.roll(z, tm - 1, 0))
    y = up * w_ref[0:1, :] + z * w_ref[1:2, :] + dn * w_ref[2:3, :]
    o_ref[...] = (cb_ref[...] * y).astype(BF16)


def _conv(tok, proj, col_block0, conv_w, layer):
    t = proj.shape[0]
    w = conv_w.shape[-1]
    tm = CONV_TILE
    assert tok.ctx_len == tm and tok.lat_len % tm == 0
    hb = tm // CONV_HALO
    last = t // CONV_HALO - 1

    def main(c):
        return pl.BlockSpec((tm, w), lambda i: (i, col_block0 + c))

    def prev(c):
        return pl.BlockSpec((CONV_HALO, w), lambda i: (jnp.maximum(i * hb - 1, 0), col_block0 + c))

    def nxt(c):
        return pl.BlockSpec((CONV_HALO, w), lambda i: (jnp.minimum((i + 1) * hb, last), col_block0 + c))

    return pl.pallas_call(
        functools.partial(_conv_kernel, tok=tok),
        grid=(t // tm,),
        in_specs=[main(0), main(1), main(2), prev(0), prev(2), nxt(0), nxt(2),
                  pl.BlockSpec((None,) + conv_w.shape[1:], lambda i: (layer, 0, 0))],
        out_specs=pl.BlockSpec((tm, w), lambda i: (i, 0)),
        out_shape=jax.ShapeDtypeStruct((t, w), BF16),
        compiler_params=_params("parallel"),
        name="conv",
    )(proj, proj, proj, proj, proj, proj, proj, conv_w)


MOE_TILE = 512
MOE_UP_TN = 512
GATHER_ROWS = 256
N_DMA_QUEUES = 2
NT_DIMS = (((1,), (1,)), ((), ()))


def _route(h, w, b):
    hh, wh = h.astype(BF16), w.astype(BF16)
    hl, wl = (h - hh.astype(F32)).astype(BF16), (w - wh.astype(F32)).astype(BF16)

    def nt(x, y):
        return lax.dot_general(x, y, NT_DIMS, preferred_element_type=F32)

    lg = nt(hh, wh) + nt(hl, wh) + nt(hh, wl) + b
    n_exp = float(lg.shape[-1])
    idx = lax.broadcasted_iota(jnp.int32, lg.shape, 1).astype(F32)
    m1 = jnp.max(lg, axis=-1, keepdims=True)
    i1 = jnp.min(jnp.where(lg == m1, idx, n_exp), axis=-1, keepdims=True)
    rest = jnp.where(idx == i1, -jnp.inf, lg)
    m2 = jnp.max(rest, axis=-1, keepdims=True)
    i2 = jnp.min(jnp.where(rest == m2, idx, n_exp), axis=-1, keepdims=True)
    e = jnp.exp(m2 - m1)
    w1 = 1.0 / (1.0 + e)
    slot = lax.broadcasted_iota(jnp.int32, (h.shape[0], TOP_K), 1)
    return jnp.where(slot == 0, i1, i2).astype(jnp.int32), jnp.where(slot == 0, w1, e * w1)


def _moe_plan(idx, n_exp):
    t, k = idx.shape
    e = idx.reshape(-1)
    onehot = (e[:, None] == jnp.arange(n_exp, dtype=jnp.int32)[None, :]).astype(jnp.int32)
    csum = jnp.cumsum(onehot, axis=0)
    counts = csum[-1]
    rank = jnp.take_along_axis(csum, e[:, None], axis=1)[:, 0] - 1
    tiles_e = (counts + MOE_TILE - 1) // MOE_TILE
    tile_end = jnp.cumsum(tiles_e)
    tile_start = tile_end - tiles_e
    n_used = tile_end[-1]
    dest = tile_start[e] * MOE_TILE + rank
    n_tiles = (t * k) // MOE_TILE + n_exp
    ti = jnp.arange(n_tiles, dtype=jnp.int32)
    tile_block = jnp.minimum(ti, n_used - 1)
    tile_expert = jnp.sum((tile_end[None, :] <= tile_block[:, None]).astype(jnp.int32), axis=1)
    tile_expert = jnp.minimum(tile_expert, n_exp - 1)
    order = jnp.argsort(e, stable=True).astype(jnp.int32)
    first = jnp.cumsum(counts) - counts
    r = jnp.arange(n_tiles * MOE_TILE, dtype=jnp.int32)
    er = tile_expert[r // MOE_TILE]
    off = r - tile_start[er] * MOE_TILE
    valid = jnp.logical_and(off < counts[er], r // MOE_TILE < n_used)
    src = jnp.where(valid, order[jnp.clip(first[er] + off, 0, t * k - 1)] // k, 0)
    return dest.astype(jnp.int32), src.astype(jnp.int32), tile_expert, tile_block.astype(jnp.int32), \
        n_used.reshape(1).astype(jnp.int32)


def _start_row_copies(idx_ref, base, n_rows, src_hbm, buf, slot, sem):
    def body(pair, carry):
        for queue in range(N_DMA_QUEUES):
            r = N_DMA_QUEUES * pair + queue
            row = idx_ref[base + r]
            pltpu.make_async_copy(src_hbm.at[pl.ds(row, 1)], buf.at[slot, pl.ds(r, 1)],
                                  sem.at[slot]).start(priority=queue)
        return carry
    lax.fori_loop(0, n_rows // N_DMA_QUEUES, body, 0, unroll=4)


def _wait_row_copies(src_hbm, buf, slot, sem):
    pltpu.make_async_copy(src_hbm.at[pl.ds(0, buf.shape[1])], buf.at[slot], sem.at[slot]).wait()


def _row_gather_pipeline(idx_ref, n_rows, src_hbm, buf, sem, n_steps):
    s = pl.program_id(0)
    slot = s % 2

    @pl.when(s == 0)
    def _():
        _start_row_copies(idx_ref, 0, n_rows, src_hbm, buf, 0, sem)

    @pl.when(s + 1 < n_steps)
    def _():
        _start_row_copies(idx_ref, (s + 1) * n_rows, n_rows, src_hbm, buf, 1 - slot, sem)

    @pl.when(s < n_steps)
    def _():
        _wait_row_copies(src_hbm, buf, slot, sem)

    return slot


def _gather_kernel(idx_ref, n_steps_ref, src_hbm, o_ref, buf, sem):
    n_steps = n_steps_ref[0]
    slot = _row_gather_pipeline(idx_ref, o_ref.shape[0], src_hbm, buf, sem, n_steps)
    used = pl.program_id(0) < n_steps

    @pl.when(used)
    def _():
        o_ref[...] = _unpack_bf16_pairs(buf[slot])

    @pl.when(jnp.logical_not(used))
    def _():
        o_ref[...] = jnp.zeros(o_ref.shape, o_ref.dtype)


def _gather_rows(src, idx, n_rows_used):
    n, dp = idx.shape[0], src.shape[1]
    d = 2 * dp
    g = GATHER_ROWS
    return pl.pallas_call(
        _gather_kernel,
        grid_spec=pltpu.PrefetchScalarGridSpec(
            num_scalar_prefetch=2, grid=(n // g,),
            in_specs=[pl.BlockSpec(memory_space=pl.ANY)],
            out_specs=pl.BlockSpec((g, d), lambda s, idx, ns: (s, 0)),
            scratch_shapes=[pltpu.VMEM((2, g, dp), src.dtype), pltpu.SemaphoreType.DMA((2,))]),
        out_shape=jax.ShapeDtypeStruct((n, d), BF16),
        compiler_params=_params("arbitrary"),
        name="moe_gather",
    )(idx, n_rows_used // g, src)


def _tile_flags(te_ref, nu_ref):
    i = pl.program_id(1)
    changed = jnp.logical_or(i == 0, te_ref[i] != te_ref[jnp.maximum(i - 1, 0)])
    return changed, i < nu_ref[0]


def _moe_up_kernel(te_ref, tb_ref, nu_ref, x_ref, wg_ref, wu_ref, o_ref, wg_bf, wu_bf, *, d_exp):
    n = pl.program_id(0)
    tn = o_ref.shape[-1]
    changed, active = _tile_flags(te_ref, nu_ref)

    @pl.when(changed)
    def _():
        row = n * tn + lax.broadcasted_iota(jnp.int32, (tn, 1), 0)
        wg_bf[...] = jnp.where(row < d_exp, wg_ref[...], 0.0).astype(BF16)
        wu_bf[...] = jnp.where(row < d_exp, wu_ref[...], 0.0).astype(BF16)

    @pl.when(active)
    def _():
        x = x_ref[...]
        g = lax.dot_general(x, wg_bf[...], NT_DIMS, preferred_element_type=F32)
        u = lax.dot_general(x, wu_bf[...], NT_DIMS, preferred_element_type=F32)
        o_ref[...] = (_silu(g) * u).astype(BF16)

    @pl.when(jnp.logical_not(active))
    def _():
        o_ref[...] = jnp.zeros(o_ref.shape, BF16)


def _moe_up(xs, plan, w_gate_t, w_up_t, layer):
    _, _, tile_expert, tile_block, n_used = plan
    d = xs.shape[1]
    d_exp = w_gate_t.shape[2]
    n_tiles = tile_expert.shape[0]
    tn = MOE_UP_TN
    nt = pl.cdiv(d_exp, tn)
    wspec = pl.BlockSpec((None, None, tn, d), lambda n, i, te, tb, nu: (layer, te[i], n, 0))
    return pl.pallas_call(
        functools.partial(_moe_up_kernel, d_exp=d_exp),
        grid_spec=pltpu.PrefetchScalarGridSpec(
            num_scalar_prefetch=3, grid=(nt, n_tiles),
            in_specs=[pl.BlockSpec((MOE_TILE, d), lambda n, i, te, tb, nu: (tb[i], 0)), wspec, wspec],
            out_specs=pl.BlockSpec((MOE_TILE, tn), lambda n, i, te, tb, nu: (i, n)),
            scratch_shapes=[pltpu.VMEM((tn, d), BF16), pltpu.VMEM((tn, d), BF16)]),
        out_shape=jax.ShapeDtypeStruct((xs.shape[0], nt * tn), BF16),
        compiler_params=_params("arbitrary", "arbitrary"),
        name="moe_up",
    )(tile_expert, tile_block, n_used, xs, w_gate_t, w_up_t)


MOE_DOWN_TN = 1024
MOE_K_ALIGN = MXU_DIM


def _moe_down_kernel(te_ref, tb_ref, nu_ref, x_ref, w_ref, o_ref, w_bf):
    changed, active = _tile_flags(te_ref, nu_ref)
    k_valid = w_ref.shape[0]

    @pl.when(changed)
    def _():
        w_bf[:k_valid, :] = w_ref[...].astype(BF16)
        w_bf[k_valid:, :] = jnp.zeros((w_bf.shape[0] - k_valid, w_bf.shape[1]), BF16)

    @pl.when(active)
    def _():
        o_ref[...] = _bdot(x_ref[...], w_bf[...])

    @pl.when(jnp.logical_not(active))
    def _():
        o_ref[...] = jnp.zeros(o_ref.shape, F32)


def _moe_down(act, plan, w_down, layer):
    _, _, tile_expert, tile_block, n_used = plan
    k_valid, n = w_down.shape[2:]
    k_pad = pl.cdiv(k_valid, MOE_K_ALIGN) * MOE_K_ALIGN
    assert k_pad <= act.shape[1]
    n_tiles = tile_expert.shape[0]
    tn = MOE_DOWN_TN
    return pl.pallas_call(
        _moe_down_kernel,
        grid_spec=pltpu.PrefetchScalarGridSpec(
            num_scalar_prefetch=3, grid=(n // tn, n_tiles),
            in_specs=[pl.BlockSpec((MOE_TILE, k_pad), lambda j, i, te, tb, nu: (tb[i], 0)),
                      pl.BlockSpec((None, None, k_valid, tn), lambda j, i, te, tb, nu: (layer, te[i], 0, j))],
            out_specs=pl.BlockSpec((MOE_TILE, tn), lambda j, i, te, tb, nu: (i, j)),
            scratch_shapes=[pltpu.VMEM((k_pad, tn), BF16)]),
        out_shape=jax.ShapeDtypeStruct((act.shape[0], n), F32),
        compiler_params=_params("arbitrary", "arbitrary"),
        name="moe_down",
    )(tile_expert, tile_block, n_used, act, w_down)


def _moe_combine_kernel(pos_ref, x_ref, wt_ref, gpost_ref, gate_ref, y_hbm, *rest, tok):
    xo_refs, (buf, sem) = rest[:-2], rest[-2:]
    tm = x_ref.shape[0]
    slot = _row_gather_pipeline(pos_ref, TOP_K * tm, y_hbm, buf, sem, pl.num_programs(0))
    wt = wt_ref[...]
    f = wt[:, 0:1] * buf[slot, pl.ds(0, tm), :]
    for k in range(1, TOP_K):
        f = f + wt[:, k:k + 1] * buf[slot, pl.ds(k * tm, tm), :]
    _store_rows(tok, xo_refs, x_ref[...] + gate_ref[...] * _rms(f, gpost_ref[...]))


def _moe_combine(tok, x, y, dest, wt, g_post, mod5, layer, seg_gate, split_out=False):
    t, d = x.shape
    tm = NORM_TILE
    pos = dest.reshape(t // tm, tm, TOP_K).transpose(0, 2, 1).reshape(-1)
    row = pl.BlockSpec((tm, d), lambda i, pos: (i, 0))
    return pl.pallas_call(
        functools.partial(_moe_combine_kernel, tok=tok),
        grid_spec=pltpu.PrefetchScalarGridSpec(
            num_scalar_prefetch=1, grid=(t // tm,),
            in_specs=[row, pl.BlockSpec((tm, TOP_K), lambda i, pos: (i, 0)),
                      pl.BlockSpec((None, 1, d), lambda i, pos: (layer, 0, 0)),
                      pl.BlockSpec((None, None, None, 1, d),
                                   lambda i, pos: (layer, tok.cond_row(i, tm), seg_gate, 0, 0)),
                      pl.BlockSpec(memory_space=pl.ANY)],
            out_specs=_split_row_specs(tok, d) if split_out else [row],
            scratch_shapes=[pltpu.VMEM((2, TOP_K * tm, d), F32), pltpu.SemaphoreType.DMA((2,))]),
        out_shape=_split_shapes(tok, d) if split_out else [jax.ShapeDtypeStruct((t, d), F32)],
        compiler_params=_params("arbitrary"),
        name="moe_combine",
    )(pos, x, wt, g_post.reshape(g_post.shape[0], 1, d), mod5, y)


SEG_SHIFT_MIX, SEG_SCALE_MIX, SEG_GATE_MIX, SEG_SHIFT_FFN, SEG_SCALE_FFN, SEG_GATE_FFN = range(N_MOD)


def kernel(x_prompt, x_sample, c, c_ctx, cache_k, cache_v, state_ssm_re, state_ssm_im, w_mod, b_mod, g_mix_pre, g_mix_post, g_ffn_pre, g_ffn_post, w_in, g_q, g_k, ssm_a_re, ssm_a_im, ssm_log_dt, ssm_b_re, ssm_b_im, ssm_c_re, ssm_c_im, ssm_d, w_glu, b_glu, conv_w, w_out, w_ffn_gate, w_ffn_up, w_ffn_down, w_router, b_router, w_exp_gate, w_exp_up, w_exp_down):
    n_ctx_seq, ctx_len, d = x_prompt.shape
    n_lat_seq, lat_len, _ = x_sample.shape
    depth = w_in.shape[0]
    tok = _Tokens(n_ctx_seq, ctx_len, n_lat_seq, lat_len)
    attn_w = d // 2
    kv_w = attn_w // Q_PER_KV
    ssm_w = ssm_d.shape[-1]
    n_kv = kv_w // HEAD_DIM
    n_dir, n_grp, n_state = ssm_a_re.shape[1:]
    nb = n_grp // SSM_GROUPS_PER_BLOCK
    sw = 2 * SSM_GROUPS_PER_BLOCK * n_state

    x = (x_prompt.reshape(tok.n_ctx, d), x_sample.reshape(tok.n_lat, d))
    cond = jnp.concatenate([c_ctx[None, :], c, jnp.zeros((8 - 1 - n_lat_seq, d), F32)], axis=0)
    mod = _modulation(cond, w_mod, b_mod)
    mod5 = mod.reshape(depth, cond.shape[0], N_MOD, 1, d)
    cos, sin = _rope_tables(tok)
    ck = cache_k.reshape(cache_k.shape[:3] + (kv_w,))
    cv = cache_v.reshape(cache_v.shape[:3] + (kv_w,))

    ssm_tables = jax.vmap(_ssm_tables)(ssm_a_re, ssm_a_im, ssm_log_dt, ssm_b_re, ssm_b_im, ssm_c_re, ssm_c_im)

    h = _prenorm(tok, x, g_mix_pre, mod5, 0, SEG_SHIFT_MIX, SEG_SCALE_MIX)
    new_k, new_v, new_re, new_im = [], [], [], []
    for l in range(depth):
        proj = _matmul(tok, [(h, 0)], w_in, l, F32)
        qn, kf, kb, vb = _qkv_prep(tok, proj, g_q, g_k, l, attn_w, kv_w, cos, sin)
        attn = (_attention_ctx(tok, qn, kb, vb), _attention_lat(tok, qn, kb, vb, ck, cv, l))
        new_k.append(kf[:tok.n_ctx].reshape(n_ctx_seq, ctx_len, n_kv, HEAD_DIM))
        new_v.append(proj[:tok.n_ctx, attn_w + kv_w:attn_w + 2 * kv_w].reshape(n_ctx_seq, ctx_len, n_kv, HEAD_DIM))

        u_col = attn_w + 2 * kv_w
        h0 = jnp.concatenate([state_ssm_re[:, l].reshape(n_lat_seq, n_dir, nb, sw // 2),
                              state_ssm_im[:, l].reshape(n_lat_seq, n_dir, nb, sw // 2)], axis=-1)
        ydir, fin = _ssm_scan(tok, proj, u_col // LANES, ssm_tables, l, h0.transpose(1, 2, 0, 3))
        fin = fin.transpose(2, 0, 1, 3).reshape(n_ctx_seq, n_dir, nb, 2, SSM_GROUPS_PER_BLOCK, n_state)
        new_re.append(fin[:, :, :, 0].reshape(n_ctx_seq, n_dir, n_grp, n_state))
        new_im.append(fin[:, :, :, 1].reshape(n_ctx_seq, n_dir, n_grp, n_state))
        ssm_out = _ssm_glu(proj, u_col // ssm_w, ydir, ssm_d, w_glu, b_glu, l)

        conv_out = _conv(tok, proj, (u_col + ssm_w) // ssm_w, conv_w, l)
        mix = _matmul(tok, [(attn, 0), (ssm_out, attn_w // ssm_w), (conv_out, attn_w // ssm_w + 1)],
                      w_out, l, F32)
        dense = l % 2 == 0
        x, h, *routing = _resid_norm(
            tok, x, mix, g_mix_post, mod5, l, SEG_GATE_MIX, nxt=(g_ffn_pre, l, SEG_SHIFT_FFN, SEG_SCALE_FFN),
            h_dtype=BF16 if dense else jnp.uint32,
            router=None if dense else (jnp.swapaxes(w_router, 1, 2), b_router, l // 2))
        last = l + 1 == depth
        nxt = None if last else (g_mix_pre, l + 1, SEG_SHIFT_MIX, SEG_SCALE_MIX)
        if dense:
            act = _swiglu_up(h, w_ffn_gate, w_ffn_up, l // 2)
            f = _down(act, w_ffn_down, l // 2)
            outs = _resid_norm(tok, x, f, g_ffn_post, mod5, l, SEG_GATE_FFN, nxt=nxt, split_out=last)
            x, h = (tuple(outs), None) if last else outs
        else:
            m = l // 2
            idx, wt = routing
            plan = _moe_plan(idx, w_router.shape[-1])
            xs = _gather_rows(h, plan[1], plan[4] * MOE_TILE)
            act = _moe_up(xs, plan, jnp.swapaxes(w_exp_gate, 2, 3), jnp.swapaxes(w_exp_up, 2, 3), m)
            y = _moe_down(act, plan, w_exp_down, m)
            outs = _moe_combine(tok, x, y, plan[0], wt, g_ffn_post, mod5, l, SEG_GATE_FFN, split_out=last)
            if last:
                x = tuple(outs)
            else:
                (x,) = outs
                h = _prenorm(tok, x, nxt[0], mod5, nxt[1], nxt[2], nxt[3])

    return (x[0].reshape(x_prompt.shape), x[1].reshape(x_sample.shape),
            jnp.stack(new_k, axis=1), jnp.stack(new_v, axis=1),
            jnp.stack(new_re, axis=1), jnp.stack(new_im, axis=1))
```
